```python
import math
import jax, jax.numpy as jnp
from jax import lax
import numpy as np

D_MODEL = 1024
BATCH = 4
SEQ = 8192
DEPTH = 1

MEM_LEN = 256
EPS = 1e-6
S5_WIDTH = 512
S5_GROUP = 16
S5_GROUPS = S5_WIDTH // S5_GROUP
S5_STATE = 64
SGU_WIDTH = 1024
SGU_HEADS = 8
SGU_HEAD_DIM = SGU_WIDTH // SGU_HEADS
CHUNK = 128
XATTN_HEADS = 4
XATTN_HEAD_DIM = D_MODEL // XATTN_HEADS
D_FF = ((8 * D_MODEL // 3 + 255) // 256) * 256
OFF_U = S5_WIDTH
OFF_V = OFF_U + SGU_WIDTH
OFF_GA = OFF_V + SGU_WIDTH
OFF_GB = OFF_GA + D_MODEL
IN_COLS = OFF_GB + D_MODEL

kernel_name = "hybrid_s5_gmlp_gated_encoder"


def rms_norm(x, g):
    xf = x.astype(jnp.float32)
    y = xf * lax.rsqrt(jnp.mean(xf * xf, axis=-1, keepdims=True) + EPS)
    return (y * g.astype(jnp.float32)).astype(x.dtype)


def layer_norm(x, g, b):
    xf = x.astype(jnp.float32)
    mu = jnp.mean(xf, axis=-1, keepdims=True)
    xc = xf - mu
    y = xc * lax.rsqrt(jnp.mean(xc * xc, axis=-1, keepdims=True) + EPS)
    return (y * g.astype(jnp.float32) + b.astype(jnp.float32)).astype(x.dtype)


def _linear_recurrence(e1, e2):
    a1, b1 = e1
    a2, b2 = e2
    return a1 * a2, a2 * b1 + b2


def s5_scan(u, lam_re, lam_im, log_step, b_re, b_im, c_re, c_im, reverse):
    f32 = jnp.float32
    seq = u.shape[0]
    lam = lax.complex(lam_re.astype(f32), lam_im.astype(f32))
    step = jnp.exp(log_step.astype(f32))[:, None]
    lam_bar = jnp.exp(lam * step)
    b = lax.complex(b_re.astype(f32), b_im.astype(f32))
    b_bar = ((lam_bar - 1.0) / lam)[..., None] * b
    bu = jnp.einsum('gpc,lbgc->lbgp', b_bar, u.astype(jnp.complex64))
    a = jnp.broadcast_to(lam_bar[None, None], (seq, 1) + lam_bar.shape)
    _, states = lax.associative_scan(_linear_recurrence, (a, bu), axis=0, reverse=reverse)
    c = lax.complex(c_re.astype(f32), c_im.astype(f32))
    return jnp.einsum('gcp,lbgp->lbgc', c, states).real


def s5_branch(xa, lam_re, lam_im, log_step, b_re, b_im, c_re, c_im, d, w_glu):
    f32 = jnp.float32
    bsz, seq, _ = xa.shape
    u = xa.astype(f32).reshape(bsz, seq, S5_GROUPS, S5_GROUP).transpose(1, 0, 2, 3)
    y = d.astype(f32).reshape(S5_GROUPS, S5_GROUP) * u
    for direction, reverse in ((0, False), (1, True)):
        y = y + s5_scan(u, lam_re[direction], lam_im[direction], log_step[direction],
                        b_re[direction], b_im[direction], c_re[direction], c_im[direction],
                        reverse)
    y = jax.nn.gelu(y.transpose(1, 0, 2, 3).reshape(bsz, seq, S5_WIDTH))
    y = y * jax.nn.sigmoid(y @ w_glu.astype(f32))
    return y.astype(xa.dtype)


def sgu_branch(zu, zv, ln_g, ln_b, w_s, bias):
    bsz, seq, _ = zu.shape
    zu = jax.nn.gelu(zu)
    zv = layer_norm(jax.nn.gelu(zv), ln_g, ln_b)
    zv = zv.reshape(bsz, seq // CHUNK, CHUNK, SGU_HEADS, SGU_HEAD_DIM)
    sv = jnp.einsum('hts,bnshd->bnthd', w_s, zv) + bias.T[None, None, :, :, None]
    return zu * sv.reshape(bsz, seq, SGU_WIDTH)


def memory_cross_attention(hn, memn, w_q, w_k, w_v, w_o):
    bsz, seq, _ = hn.shape
    q = (hn @ w_q).reshape(bsz, seq, XATTN_HEADS, XATTN_HEAD_DIM)
    k = (memn @ w_k).reshape(bsz, MEM_LEN, XATTN_HEADS, XATTN_HEAD_DIM)
    v = (memn @ w_v).reshape(bsz, MEM_LEN, XATTN_HEADS, XATTN_HEAD_DIM)
    s = jnp.einsum('blhd,bmhd->bhlm', q, k).astype(jnp.float32) * (XATTN_HEAD_DIM ** -0.5)
    p = jax.nn.softmax(s, axis=-1).astype(v.dtype)
    o = jnp.einsum('bhlm,bmhd->blhd', p, v).reshape(bsz, seq, D_MODEL)
    return o @ w_o


def swiglu(hn, w_gate, w_up, w_down):
    return (jax.nn.silu(hn @ w_gate) * (hn @ w_up)) @ w_down


def setup_inputs(seed: int = 0) -> dict:
    key = jax.random.key(seed)
    ks = iter(jax.random.split(key, 40))
    f32 = jnp.float32

    def nrm(shape, scale):
        return jax.random.normal(next(ks), shape, f32) * scale

    def gain(shape):
        return 1.0 + nrm(shape, 0.02)

    L, G, P, C, H = DEPTH, S5_GROUPS, S5_STATE, S5_GROUP, SGU_HEADS
    n_idx = jnp.arange(P, dtype=f32)
    lam_re = -0.5 + nrm((L, 2, G, P), 0.01)
    lam_im = math.pi * n_idx + nrm((L, 2, G, P), 0.01)
    log_step = jax.random.uniform(next(ks), (L, 2, G), f32, math.log(1e-3), math.log(1e-1))
    return {
        "x": nrm((BATCH, SEQ, D_MODEL), 1.0),
        "mem": nrm((BATCH, MEM_LEN, D_MODEL), 1.0),
        "mix_norm_g": gain((L, D_MODEL)),
        "w_in": nrm((L, D_MODEL, IN_COLS), D_MODEL ** -0.5),
        "s5_lam_re": lam_re,
        "s5_lam_im": lam_im,
        "s5_log_step": log_step,
        "s5_b_re": nrm((L, 2, G, P, C), (2 * C) ** -0.5),
        "s5_b_im": nrm((L, 2, G, P, C), (2 * C) ** -0.5),
        "s5_c_re": nrm((L, 2, G, C, P), (2 * P) ** -0.5),
        "s5_c_im": nrm((L, 2, G, C, P), (2 * P) ** -0.5),
        "s5_d": nrm((L, S5_WIDTH), 1.0),
        "s5_w_glu": nrm((L, S5_WIDTH, S5_WIDTH), S5_WIDTH ** -0.5),
        "sgu_ln_g": gain((L, SGU_WIDTH)),
        "sgu_ln_b": nrm((L, SGU_WIDTH), 0.02),
        "sgu_w": nrm((L, H, CHUNK, CHUNK), 0.5 * CHUNK ** -0.5),
        "sgu_bias": 1.0 + nrm((L, H, CHUNK), 0.02),
        "w_proj_a": nrm((L, S5_WIDTH, D_MODEL), S5_WIDTH ** -0.5),
        "w_proj_b": nrm((L, SGU_WIDTH, D_MODEL), SGU_WIDTH ** -0.5),
        "w_out": nrm((L, D_MODEL, D_MODEL), D_MODEL ** -0.5),
        "xattn_norm_g": gain((L, D_MODEL)),
        "mem_norm_g": gain((D_MODEL,)),
        "w_q": nrm((L, D_MODEL, D_MODEL), D_MODEL ** -0.5),
        "w_k": nrm((L, D_MODEL, D_MODEL), D_MODEL ** -0.5),
        "w_v": nrm((L, D_MODEL, D_MODEL), D_MODEL ** -0.5),
        "w_xo": nrm((L, D_MODEL, D_MODEL), D_MODEL ** -0.5),
        "ffn_norm_g": gain((L, D_MODEL)),
        "w_gate": nrm((L, D_MODEL, D_FF), D_MODEL ** -0.5),
        "w_up": nrm((L, D_MODEL, D_FF), D_MODEL ** -0.5),
        "w_down": nrm((L, D_FF, D_MODEL), D_FF ** -0.5),
        "final_norm_g": gain((D_MODEL,)),
    }


def reference(x, mem, mix_norm_g, w_in, s5_lam_re, s5_lam_im, s5_log_step, s5_b_re, s5_b_im,
              s5_c_re, s5_c_im, s5_d, s5_w_glu, sgu_ln_g, sgu_ln_b, sgu_w, sgu_bias,
              w_proj_a, w_proj_b, w_out, xattn_norm_g, mem_norm_g, w_q, w_k, w_v, w_xo,
              ffn_norm_g, w_gate, w_up, w_down, final_norm_g):
    memn = rms_norm(mem, mem_norm_g)
    h = x
    for i in range(DEPTH):
        n = rms_norm(h, mix_norm_g[i])
        proj = n @ w_in[i]
        xa = proj[..., :OFF_U]
        zu = proj[..., OFF_U:OFF_V]
        zv = proj[..., OFF_V:OFF_GA]
        gate_a = jax.nn.sigmoid(proj[..., OFF_GA:OFF_GB])
        gate_b = jax.nn.sigmoid(proj[..., OFF_GB:])
        ya = s5_branch(xa, s5_lam_re[i], s5_lam_im[i], s5_log_step[i], s5_b_re[i], s5_b_im[i],
                       s5_c_re[i], s5_c_im[i], s5_d[i], s5_w_glu[i])
        yb = sgu_branch(zu, zv, sgu_ln_g[i], sgu_ln_b[i], sgu_w[i], sgu_bias[i])
        merged = gate_a * (ya @ w_proj_a[i]) + gate_b * (yb @ w_proj_b[i])
        h = h + merged @ w_out[i]
        h = h + memory_cross_attention(rms_norm(h, xattn_norm_g[i]), memn,
                                       w_q[i], w_k[i], w_v[i], w_xo[i])
        h = h + swiglu(rms_norm(h, ffn_norm_g[i]), w_gate[i], w_up[i], w_down[i])
    return rms_norm(h, final_norm_g)
```

```python
import functools

import jax
import jax.numpy as jnp
from jax import lax
from jax.experimental import pallas as pl
from jax.experimental.pallas import tpu as pltpu

F32 = jnp.float32
BF16 = jnp.bfloat16

EPS = 1e-6
LANES = 128
CHUNK = 128
S5_GROUP = 16
S5_STATE = 64
SGU_HEADS = 8
XATTN_HEADS = 4
VMEM_LIMIT = 56 * 1024 * 1024


def _cparams(*sem):
    return pltpu.CompilerParams(dimension_semantics=sem, vmem_limit_bytes=VMEM_LIMIT)


def _const_spec(shape):
    nd = len(shape)
    return pl.BlockSpec(shape, lambda *_: (0,) * nd, pipeline_mode=pl.Buffered(1))


def _rms(x, g):
    return x * lax.rsqrt(jnp.mean(x * x, axis=-1, keepdims=True) + EPS) * g


def _dot(a, b):
    return jnp.dot(a, b, preferred_element_type=F32)


def _dot_nt(a, b):
    return lax.dot_general(a, b, (((1,), (1,)), ((), ())), preferred_element_type=F32)


def _mem_kv_kernel(mem_ref, g_ref, wk_ref, wv_ref, k_ref, v_ref, *, scale):
    mn = _rms(mem_ref[0], g_ref[...]).astype(BF16)
    k_ref[0] = (_dot(mn, wk_ref[...]) * scale).astype(BF16)
    v_ref[0] = _dot(mn, wv_ref[...]).astype(BF16)


def _mem_kv(mem, g, wk, wv, scale):
    b, m, d = mem.shape
    return pl.pallas_call(
        functools.partial(_mem_kv_kernel, scale=scale),
        grid=(b,),
        in_specs=[pl.BlockSpec((1, m, d), lambda i: (i, 0, 0)),
                  _const_spec((1, d)), _const_spec((d, d)), _const_spec((d, d))],
        out_specs=[pl.BlockSpec((1, m, d), lambda i: (i, 0, 0))] * 2,
        out_shape=[jax.ShapeDtypeStruct((b, m, d), BF16)] * 2,
        compiler_params=_cparams("arbitrary"),
        name="mem_kv",
    )(mem, g, wk, wv)


def _in_proj_kernel(x_ref, g_ref, wat_ref, wu_ref, wv_ref, wga_ref, wgb_ref, lng_ref, lnb_ref,
                    ws_ref, bias_ref, wpb_ref, xat_ref, ga_ref, pb_ref):
    tm = x_ref.shape[0]
    n = _rms(x_ref[...], g_ref[...]).astype(BF16)
    xat_ref[...] = _dot_nt(wat_ref[...], n).astype(BF16)
    ga_ref[...] = jax.nn.sigmoid(_dot(n, wga_ref[...])).astype(BF16)
    zv = jax.nn.gelu(_dot(n, wv_ref[...]))
    zc = zv - jnp.mean(zv, axis=-1, keepdims=True)
    zvn = (zc * lax.rsqrt(jnp.mean(zc * zc, axis=-1, keepdims=True) + EPS) * lng_ref[...]
           + lnb_ref[...]).astype(BF16)
    hd = zvn.shape[1] // SGU_HEADS
    rows = []
    for c in range(tm // CHUNK):
        blk = zvn[c * CHUNK:(c + 1) * CHUNK]
        rows.append(jnp.concatenate(
            [_dot(ws_ref[h], blk[:, h * hd:(h + 1) * hd]) for h in range(SGU_HEADS)], axis=1)
            + bias_ref[...])
    sv = jnp.concatenate(rows, axis=0)
    zu = jax.nn.gelu(_dot(n, wu_ref[...]))
    yb = (zu * sv).astype(BF16)
    gb = jax.nn.sigmoid(_dot(n, wgb_ref[...]))
    pb_ref[...] = (gb * _dot(yb, wpb_ref[...])).astype(BF16)


def _in_proj(x2, g, wat, wu, wv, wga, wgb, lng, lnb, ws, bias_full, wpb, tm):
    t, d = x2.shape
    sw = wat.shape[0]
    consts = (g, wat, wu, wv, wga, wgb, lng, lnb, ws, bias_full, wpb)
    return pl.pallas_call(
        _in_proj_kernel,
        grid=(t // tm,),
        in_specs=[pl.BlockSpec((tm, d), lambda i: (i, 0))] + [_const_spec(c.shape) for c in consts],
        out_specs=[pl.BlockSpec((sw, tm), lambda i: (0, i)),
                   pl.BlockSpec((tm, d), lambda i: (i, 0)),
                   pl.BlockSpec((tm, d), lambda i: (i, 0))],
        out_shape=[jax.ShapeDtypeStruct((sw, t), BF16),
                   jax.ShapeDtypeStruct((t, d), BF16),
                   jax.ShapeDtypeStruct((t, d), BF16)],
        compiler_params=_cparams("arbitrary"),
        name="in_proj",
    )(x2, *consts)


def _s5_kernel(x_ref, cbk_ref, ptab_ref, dcol_ref, lt_ref, bb_ref, et_ref, cc_ref, a_ref,
               y_ref, m_ref, cm_ref, kv_ref, s_ref, xf_ref, xb_ref, *, nbatch):
    gc = S5_GROUP
    nrows = x_ref.shape[1]
    nch = nrows // nbatch
    ns = 2 * S5_STATE

    for dr in range(2):
        kv_ref[dr] = jnp.dot(cbk_ref[0, dr], ptab_ref[0, dr], preferred_element_type=F32,
                             precision=lax.Precision.HIGHEST)
    lane = lax.broadcasted_iota(jnp.int32, (gc * gc, CHUNK), 1)
    kv_ref[0] = kv_ref[0] + jnp.where(lane == 0, kv_ref[1][:, 0:1] + dcol_ref[0], 0.0)

    row = lax.broadcasted_iota(jnp.int32, (CHUNK, CHUNK), 0)
    col = lax.broadcasted_iota(jnp.int32, (CHUNK, CHUNK), 1)
    upper = col >= row

    lt_re, lt_im = lt_ref[0, 0], lt_ref[0, 1]
    et_re, et_im = et_ref[0, 0], et_ref[0, 1]

    def build(ci, carry):
        r0 = pl.multiple_of(ci * CHUNK, CHUNK)
        for co in range(gc):
            vf = jnp.broadcast_to(kv_ref[0, pl.ds(ci * gc + co, 1), :], (CHUNK, CHUNK))
            vb = jnp.broadcast_to(kv_ref[1, pl.ds(ci * gc + co, 1), :], (CHUNK, CHUNK))
            cf = pltpu.roll(vf, 0, 1, stride=1, stride_axis=0)
            cb = pltpu.roll(vb, 0, 1, stride=1, stride_axis=0)
            m_ref[pl.ds(r0, CHUNK), co * CHUNK:(co + 1) * CHUNK] = jnp.where(upper, cf, cb).astype(BF16)
        b_re = bb_ref[0, 0, pl.ds(ci, 1), :]
        b_im = bb_ref[0, 1, pl.ds(ci, 1), :]
        w_re = lt_re * b_re - lt_im * b_im
        w_im = lt_re * b_im + lt_im * b_re
        m_ref[pl.ds(r0, CHUNK), gc * CHUNK:gc * CHUNK + 2 * ns] = (
            jnp.concatenate([w_re, w_im], axis=1).astype(BF16))
        c_re = cc_ref[0, 0, pl.ds(ci, 1), :]
        c_im = cc_ref[0, 1, pl.ds(ci, 1), :]
        g_re = et_re * c_re - et_im * c_im
        g_im = et_re * c_im + et_im * c_re
        cm_ref[pl.ds(r0, CHUNK), :] = jnp.concatenate([g_re, -g_im], axis=1).astype(BF16)
        return carry

    lax.fori_loop(0, gc, build, 0)

    x = jnp.concatenate([x_ref[ci] for ci in range(gc)], axis=1)
    res = _dot(x, m_ref[...])
    y0 = res[:, :gc * CHUNK]
    s_ref[0] = res[:, gc * CHUNK:gc * CHUNK + ns]
    s_ref[1] = res[:, gc * CHUNK + ns:]

    a_re, a_im = a_ref[0, 0], a_ref[0, 1]
    is_fwd = lax.broadcasted_iota(jnp.int32, (nbatch, ns), 1) < S5_STATE

    def step(i, st):
        st_re, st_im = st
        rows_f = pl.ds(i, nbatch, stride=nch)
        rows_b = pl.ds(nch - 1 - i, nbatch, stride=nch)
        xf_ref[0, rows_f, :] = st_re
        xf_ref[1, rows_f, :] = st_im
        xb_ref[0, rows_b, :] = st_re
        xb_ref[1, rows_b, :] = st_im
        s_re = jnp.where(is_fwd, s_ref[0, rows_f, :], s_ref[0, rows_b, :])
        s_im = jnp.where(is_fwd, s_ref[1, rows_f, :], s_ref[1, rows_b, :])
        return (a_re * st_re - a_im * st_im + s_re, a_re * st_im + a_im * st_re + s_im)

    zero = jnp.zeros((nbatch, ns), F32)
    lax.fori_loop(0, nch, step, (zero, zero))
    is_fwd_all = lax.broadcasted_iota(jnp.int32, (nrows, ns), 1) < S5_STATE
    xin = jnp.concatenate([jnp.where(is_fwd_all, xf_ref[0], xb_ref[0]),
                           jnp.where(is_fwd_all, xf_ref[1], xb_ref[1])], axis=1).astype(BF16)
    y = y0 + _dot_nt(xin, cm_ref[...])
    for co in range(gc):
        y_ref[co] = y[:, co * CHUNK:(co + 1) * CHUNK]


def _s5(xat3, tabs, nbatch):
    sw, nrows, _ = xat3.shape
    ngroups = sw // S5_GROUP
    gc, ns = S5_GROUP, 2 * S5_STATE

    def gspec(a):
        nd = a.ndim
        return pl.BlockSpec((1,) + a.shape[1:], lambda g: (g,) + (0,) * (nd - 1))

    return pl.pallas_call(
        functools.partial(_s5_kernel, nbatch=nbatch),
        grid=(ngroups,),
        in_specs=[pl.BlockSpec((gc, nrows, CHUNK), lambda g: (g, 0, 0))] + [gspec(a) for a in tabs],
        out_specs=pl.BlockSpec((gc, nrows, CHUNK), lambda g: (g, 0, 0)),
        out_shape=jax.ShapeDtypeStruct((sw, nrows, CHUNK), F32),
        scratch_shapes=[pltpu.VMEM((gc * CHUNK, gc * CHUNK + 2 * ns), BF16),
                        pltpu.VMEM((gc * CHUNK, 2 * ns), BF16),
                        pltpu.VMEM((2, gc * gc, CHUNK), F32),
                        pltpu.VMEM((2, nrows, ns), F32),
                        pltpu.VMEM((2, nrows, ns), F32),
                        pltpu.VMEM((2, nrows, ns), F32)],
        compiler_params=_cparams("arbitrary"),
        name="s5",
    )(xat3, *tabs)


def _s5_tables(lam_re, lam_im, log_step, b_re, b_im, c_re, c_im, d):
    ng, npar, nc = b_re.shape[1], b_re.shape[2], b_re.shape[3]
    lam = lax.complex(lam_re.astype(F32), lam_im.astype(F32))
    z = lam * jnp.exp(log_step.astype(F32))[..., None]
    lam_bar = jnp.exp(z)
    b_bar = ((lam_bar - 1.0) / lam)[..., None] * lax.complex(b_re.astype(F32), b_im.astype(F32))
    c = lax.complex(c_re.astype(F32), c_im.astype(F32))
    n = jnp.arange(CHUNK + 1, dtype=F32)
    pw = jnp.exp(z[..., None] * n)

    def reim(a, axis):
        return jnp.stack([jnp.real(a), jnp.imag(a)], axis=axis)

    cb = jnp.einsum('xgop,xgpi->xgiop', c, b_bar).reshape(2, ng, nc * nc, npar)
    cbk = jnp.concatenate([jnp.real(cb), -jnp.imag(cb)], axis=-1)
    cbk = cbk.transpose(1, 0, 2, 3)
    idx_b = (CHUNK - jnp.arange(CHUNK)) % CHUNK
    pf = pw[0, :, :, :CHUNK]
    pb = pw[1][:, :, idx_b]
    ptab = jnp.stack([jnp.concatenate([jnp.real(pf), jnp.imag(pf)], axis=1),
                      jnp.concatenate([jnp.real(pb), jnp.imag(pb)], axis=1)], axis=1)
    eye = jnp.eye(nc, dtype=F32)
    dcol = (d.astype(F32).reshape(ng, nc, 1) * eye[None]).reshape(ng, nc * nc, 1)

    ltf = pw[0, :, :, :CHUNK][:, :, ::-1]
    ltb = pw[1, :, :, :CHUNK]
    lt = jnp.concatenate([ltf, ltb], axis=1).transpose(0, 2, 1)
    lt = reim(lt, 1)
    bb = jnp.concatenate([b_bar[0], b_bar[1]], axis=1).transpose(0, 2, 1)
    bb = reim(bb, 1)
    etf = pw[0, :, :, 1:CHUNK + 1]
    etb = pw[1, :, :, 1:CHUNK + 1][:, :, ::-1]
    et = reim(jnp.concatenate([etf, etb], axis=1).transpose(0, 2, 1), 1)
    cc = reim(jnp.concatenate([c[0], c[1]], axis=2), 1)
    a = jnp.concatenate([pw[0, :, :, CHUNK], pw[1, :, :, CHUNK]], axis=1)[:, None, :]
    a = reim(a, 1)
    return cbk, ptab, dcol, lt, bb, et, cc, a


def _merge_kernel(yt_ref, ga_ref, pb_ref, x_ref, wglut_ref, wpa_ref, wout_ref, h_ref):
    y = jax.nn.gelu(yt_ref[...])
    gl = _dot(wglut_ref[...], y.astype(BF16))
    ya = (y * jax.nn.sigmoid(gl)).T.astype(BF16)
    merged = ga_ref[...].astype(F32) * _dot(ya, wpa_ref[...]) + pb_ref[...].astype(F32)
    h_ref[...] = x_ref[...] + _dot(merged.astype(BF16), wout_ref[...])


def _merge(yt, ga, pb, x2, wglut, wpa, wout, tm):
    t, d = x2.shape
    sw = yt.shape[0]
    return pl.pallas_call(
        _merge_kernel,
        grid=(t // tm,),
        in_specs=[pl.BlockSpec((sw, tm), lambda i: (0, i)),
                  pl.BlockSpec((tm, d), lambda i: (i, 0)),
                  pl.BlockSpec((tm, d), lambda i: (i, 0)),
                  pl.BlockSpec((tm, d), lambda i: (i, 0)),
                  _const_spec(wglut.shape), _const_spec(wpa.shape), _const_spec(wout.shape)],
        out_specs=pl.BlockSpec((tm, d), lambda i: (i, 0)),
        out_shape=jax.ShapeDtypeStruct((t, d), F32),
        compiler_params=_cparams("arbitrary"),
        name="merge",
    )(yt, ga, pb, x2, wglut, wpa, wout)


def _xattn_kernel(h_ref, g_ref, wq_ref, k_ref, v_ref, wo_ref, o_ref):
    h = h_ref[0]
    hn = _rms(h, g_ref[...]).astype(BF16)
    q = _dot(hn, wq_ref[...]).astype(BF16)
    hd = q.shape[1] // XATTN_HEADS
    outs = []
    for hh in range(XATTN_HEADS):
        sl = slice(hh * hd, (hh + 1) * hd)
        s = _dot_nt(q[:, sl], k_ref[0, :, sl])
        e = jnp.exp(s - jnp.max(s, axis=-1, keepdims=True))
        p = e / jnp.sum(e, axis=-1, keepdims=True)
        outs.append(_dot(p.astype(BF16), v_ref[0, :, sl]))
    o = jnp.concatenate(outs, axis=1).astype(BF16)
    o_ref[0] = h + _dot(o, wo_ref[...])


def _xattn(h3, g, wq, k, v, wo, tm):
    b, l, d = h3.shape
    m = k.shape[1]
    return pl.pallas_call(
        _xattn_kernel,
        grid=(b, l // tm),
        in_specs=[pl.BlockSpec((1, tm, d), lambda i, j: (i, j, 0)),
                  _const_spec(g.shape), _const_spec(wq.shape),
                  pl.BlockSpec((1, m, d), lambda i, j: (i, 0, 0)),
                  pl.BlockSpec((1, m, d), lambda i, j: (i, 0, 0)),
                  _const_spec(wo.shape)],
        out_specs=pl.BlockSpec((1, tm, d), lambda i, j: (i, j, 0)),
        out_shape=jax.ShapeDtypeStruct((b, l, d), F32),
        compiler_params=_cparams("arbitrary", "arbitrary"),
        name="xattn",
    )(h3, g, wq, k, v, wo)


def _ffn_kernel(h_ref, g_ref, wg_ref, wu_ref, wd_ref, gf_ref, o_ref):
    h = h_ref[...]
    hn = _rms(h, g_ref[...]).astype(BF16)
    act = (jax.nn.silu(_dot(hn, wg_ref[...])) * _dot(hn, wu_ref[...])).astype(BF16)
    o_ref[...] = _rms(h + _dot(act, wd_ref[...]), gf_ref[...])


def _ffn(h2, g, wg, wu, wd, gf, tm):
    t, d = h2.shape
    consts = (g, wg, wu, wd, gf)
    return pl.pallas_call(
        _ffn_kernel,
        grid=(t // tm,),
        in_specs=[pl.BlockSpec((tm, d), lambda i: (i, 0))] + [_const_spec(c.shape) for c in consts],
        out_specs=pl.BlockSpec((tm, d), lambda i: (i, 0)),
        out_shape=jax.ShapeDtypeStruct((t, d), F32),
        compiler_params=_cparams("arbitrary"),
        name="ffn",
    )(h2, *consts)


def _layer(h, kmem, vmem, mix_g, w_in, s5p, w_glu, ln_g, ln_b, sgu_w, sgu_bias, wpa, wpb, wout,
           xg, wq, wxo, fg, wgate, wup, wdown, final_g, tm):
    b, l, d = h.shape
    t = b * l
    sw = w_glu.shape[0]
    su = ln_g.shape[0]
    off_u, off_v, off_ga, off_gb = sw, sw + su, sw + 2 * su, sw + 2 * su + d
    row = lambda a: a.astype(F32).reshape(1, -1)
    x2 = h.reshape(t, d)
    wb = w_in.astype(BF16)
    bias_full = jnp.repeat(sgu_bias.astype(F32).T, su // SGU_HEADS, axis=1)
    xat, ga, pb = _in_proj(
        x2, row(mix_g), wb[:, :off_u].T, wb[:, off_u:off_v], wb[:, off_v:off_ga],
        wb[:, off_ga:off_gb], wb[:, off_gb:], row(ln_g), row(ln_b), sgu_w.astype(BF16),
        bias_full, wpb.astype(BF16), tm)
    tabs = _s5_tables(*s5p)
    yt = _s5(xat.reshape(sw, t // CHUNK, CHUNK), tabs, b).reshape(sw, t)
    h1 = _merge(yt, ga, pb, x2, w_glu.astype(BF16).T, wpa.astype(BF16), wout.astype(BF16), tm)
    h2 = _xattn(h1.reshape(b, l, d), row(xg), wq.astype(BF16), kmem, vmem, wxo.astype(BF16), tm)
    out = _ffn(h2.reshape(t, d), row(fg), wgate.astype(BF16), wup.astype(BF16), wdown.astype(BF16),
               row(final_g), tm)
    return out.reshape(b, l, d)


def kernel(x, mem, mix_norm_g, w_in, s5_lam_re, s5_lam_im, s5_log_step, s5_b_re, s5_b_im, s5_c_re, s5_c_im, s5_d, s5_w_glu, sgu_ln_g, sgu_ln_b, sgu_w, sgu_bias, w_proj_a, w_proj_b, w_out, xattn_norm_g, mem_norm_g, w_q, w_k, w_v, w_xo, ffn_norm_g, w_gate, w_up, w_down, final_norm_g):
    depth = w_in.shape[0]
    assert depth == 1, "the final rms_norm is fused into the last layer's ffn kernel"
    hd = x.shape[-1] // XATTN_HEADS
    i = 0
    kmem, vmem = _mem_kv(mem, mem_norm_g.astype(F32).reshape(1, -1), w_k[i].astype(BF16),
                         w_v[i].astype(BF16), hd ** -0.5)
    s5p = (s5_lam_re[i], s5_lam_im[i], s5_log_step[i], s5_b_re[i], s5_b_im[i], s5_c_re[i],
           s5_c_im[i], s5_d[i])
    return _layer(x, kmem, vmem, mix_norm_g[i], w_in[i], s5p, s5_w_glu[i], sgu_ln_g[i], sgu_ln_b[i],
                  sgu_w[i], sgu_bias[i], w_proj_a[i], w_proj_b[i], w_out[i], xattn_norm_g[i], w_q[i],
                  w_xo[i], ffn_norm_g[i], w_gate[i], w_up[i], w_down[i], final_norm_g, tm=256)
```

```python
import functools

import jax
import jax.numpy as jnp
from jax import lax
from jax.experimental import pallas as pl
from jax.experimental.pallas import tpu as pltpu

F32 = jnp.float32
BF16 = jnp.bfloat16

EPS = 1e-6
LANES = 128
CHUNK = 128
S5_GROUP = 16
S5_STATE = 64
SGU_HEADS = 8
XATTN_HEADS = 4
VMEM_LIMIT = 56 * 1024 * 1024
TOKEN_BLOCK = 1024
TOKEN_SUB = 256


def _cparams(*sem):
    return pltpu.CompilerParams(dimension_semantics=sem, vmem_limit_bytes=VMEM_LIMIT)


def _const_spec(shape):
    nd = len(shape)
    return pl.BlockSpec(shape, lambda *_: (0,) * nd, pipeline_mode=pl.Buffered(1))


def _rms(x, g):
    return x * lax.rsqrt(jnp.mean(x * x, axis=-1, keepdims=True) + EPS) * g


def _dot(a, b):
    return jnp.dot(a, b, preferred_element_type=F32)


def _dot_nt(a, b):
    return lax.dot_general(a, b, (((1,), (1,)), ((), ())), preferred_element_type=F32)


def _sub_loop(n_tokens, body):
    def step(sub, carry):
        body(pl.multiple_of(sub * TOKEN_SUB, TOKEN_SUB), sub)
        return carry
    lax.fori_loop(0, n_tokens // TOKEN_SUB, step, 0)


def _mem_kv_kernel(mem_ref, g_ref, wk_ref, wv_ref, k_ref, v_ref, *, scale):
    mn = _rms(mem_ref[0], g_ref[...]).astype(BF16)
    k_ref[0] = (_dot(mn, wk_ref[...]) * scale).astype(BF16)
    v_ref[0] = _dot(mn, wv_ref[...]).astype(BF16)


def _mem_kv(mem, g, wk, wv, scale):
    b, m, d = mem.shape
    return pl.pallas_call(
        functools.partial(_mem_kv_kernel, scale=scale),
        grid=(b,),
        in_specs=[pl.BlockSpec((1, m, d), lambda i: (i, 0, 0)),
                  _const_spec((1, d)), _const_spec((d, d)), _const_spec((d, d))],
        out_specs=[pl.BlockSpec((1, m, d), lambda i: (i, 0, 0))] * 2,
        out_shape=[jax.ShapeDtypeStruct((b, m, d), BF16)] * 2,
        compiler_params=_cparams("arbitrary"),
        name="mem_kv",
    )(mem, g, wk, wv)


def _in_proj_kernel(x_ref, g_ref, wat_ref, wu_ref, wv_ref, wga_ref, wgb_ref, lng_ref, lnb_ref,
                    ws_ref, bias_ref, wpb_ref, xat_ref, ga_ref, pb_ref):
    def body(r0, sub):
        rows = pl.ds(r0, TOKEN_SUB)
        n = _rms(x_ref[rows, :], g_ref[...]).astype(BF16)
        xat = _dot_nt(wat_ref[...], n)
        for c in range(TOKEN_SUB // CHUNK):
            xat_ref[:, sub * (TOKEN_SUB // CHUNK) + c, :] = xat[:, c * CHUNK:(c + 1) * CHUNK]
        ga_ref[rows, :] = jax.nn.sigmoid(_dot(n, wga_ref[...])).astype(BF16)
        zv = jax.nn.gelu(_dot(n, wv_ref[...]))
        zc = zv - jnp.mean(zv, axis=-1, keepdims=True)
        zvn = (zc * lax.rsqrt(jnp.mean(zc * zc, axis=-1, keepdims=True) + EPS) * lng_ref[...]
               + lnb_ref[...]).astype(BF16)
        hd = zvn.shape[1] // SGU_HEADS
        chunks = []
        for c in range(TOKEN_SUB // CHUNK):
            blk = zvn[c * CHUNK:(c + 1) * CHUNK]
            chunks.append(jnp.concatenate(
                [_dot(ws_ref[h], blk[:, h * hd:(h + 1) * hd]) for h in range(SGU_HEADS)], axis=1)
                + bias_ref[...])
        sv = jnp.concatenate(chunks, axis=0)
        zu = jax.nn.gelu(_dot(n, wu_ref[...]))
        yb = (zu * sv).astype(BF16)
        gb = jax.nn.sigmoid(_dot(n, wgb_ref[...]))
        pb_ref[rows, :] = (gb * _dot(yb, wpb_ref[...])).astype(BF16)

    _sub_loop(x_ref.shape[0], body)


def _in_proj(x2, g, wat, wu, wv, wga, wgb, lng, lnb, ws, bias_full, wpb):
    t, d = x2.shape
    sw = wat.shape[0]
    tm = TOKEN_BLOCK
    consts = (g, wat, wu, wv, wga, wgb, lng, lnb, ws, bias_full, wpb)
    return pl.pallas_call(
        _in_proj_kernel,
        grid=(t // tm,),
        in_specs=[pl.BlockSpec((tm, d), lambda i: (i, 0))] + [_const_spec(c.shape) for c in consts],
        out_specs=[pl.BlockSpec((sw, tm // CHUNK, CHUNK), lambda i: (0, i, 0)),
                   pl.BlockSpec((tm, d), lambda i: (i, 0)),
                   pl.BlockSpec((tm, d), lambda i: (i, 0))],
        out_shape=[jax.ShapeDtypeStruct((sw, t // CHUNK, CHUNK), F32),
                   jax.ShapeDtypeStruct((t, d), BF16),
                   jax.ShapeDtypeStruct((t, d), BF16)],
        compiler_params=_cparams("arbitrary"),
        name="in_proj",
    )(x2, *consts)


def _cmul(ar, ai, br, bi):
    return ar * br - ai * bi, ar * bi + ai * br


def _cpow(br, bi, e, nbits):
    rr = ri = None
    for k in range(nbits):
        bit = ((e >> k) & 1) == 1
        sr, si = jnp.where(bit, br, 1.0), jnp.where(bit, bi, 0.0)
        rr, ri = (sr, si) if rr is None else _cmul(rr, ri, sr, si)
        if k + 1 < nbits:
            br, bi = _cmul(br, bi, br, bi)
    return rr, ri


def _s5_kernel(x_ref, row_ref, col_ref, braw_ref, c_ref, dcol_ref,
               y_ref, m_ref, cm_ref, kv_ref, s_ref, xf_ref, xb_ref, *, nbatch):
    gc = S5_GROUP
    nrows = x_ref.shape[1]
    nch = nrows // nbatch
    ns = 2 * S5_STATE
    sq = (CHUNK, CHUNK)
    sub = lax.broadcasted_iota(jnp.int32, sq, 0)
    lan = lax.broadcasted_iota(jnp.int32, sq, 1)

    lb_re, lb_im = row_ref[0, 0:1, :], row_ref[0, 1:2, :]
    q_re, q_im = row_ref[0, 2:3, :], row_ref[0, 3:4, :]
    fwd_lane = lan < S5_STATE
    lt_re, lt_im = _cpow(lb_re, lb_im, jnp.where(fwd_lane, CHUNK - 1 - sub, sub), 7)
    et_re, et_im = _cpow(lb_re, lb_im, jnp.where(fwd_lane, sub + 1, CHUNK - sub), 8)
    p_re, p_im = _cpow(col_ref[0, :, 0:1], col_ref[0, :, 1:2],
                       jnp.where(sub < S5_STATE, lan, (CHUNK - lan) & (CHUNK - 1)), 7)
    a_re, a_im = lb_re, lb_im
    for _ in range(7):
        a_re, a_im = _cmul(a_re, a_im, a_re, a_im)

    bb_re, bb_im = _cmul(q_re, q_im, braw_ref[0, 0], braw_ref[0, 1])
    c_re, c_im = c_ref[0, 0], c_ref[0, 1]

    cb = [_cmul(c_re, c_im, bb_re[ci:ci + 1], bb_im[ci:ci + 1]) for ci in range(gc)]
    cb_re = jnp.concatenate([v[0] for v in cb], axis=0)
    cb_im = jnp.concatenate([v[1] for v in cb], axis=0)
    lane256 = lax.broadcasted_iota(jnp.int32, (gc * gc, ns), 1) < S5_STATE
    lhs_f = jnp.where(lane256, cb_re, -pltpu.roll(cb_im, S5_STATE, 1))
    lhs_b = jnp.where(lane256, pltpu.roll(cb_re, S5_STATE, 1), -cb_im)
    rhs_f = jnp.concatenate([p_re[:S5_STATE], p_im[:S5_STATE]], axis=0)
    rhs_b = jnp.concatenate([p_re[S5_STATE:], p_im[S5_STATE:]], axis=0)
    kf = jnp.dot(lhs_f, rhs_f, preferred_element_type=F32, precision=lax.Precision.HIGHEST)
    kb = jnp.dot(lhs_b, rhs_b, preferred_element_type=F32, precision=lax.Precision.HIGHEST)
    lane_tap = lax.broadcasted_iota(jnp.int32, (gc * gc, CHUNK), 1)
    kv_ref[0] = kf + jnp.where(lane_tap == 0, kb[:, 0:1] + dcol_ref[0], 0.0)
    kv_ref[1] = kb

    upper = lan >= sub

    def build(ci, carry):
        r0 = pl.multiple_of(ci * CHUNK, CHUNK)
        for co in range(gc):
            vf = jnp.broadcast_to(kv_ref[0, pl.ds(ci * gc + co, 1), :], sq)
            vb = jnp.broadcast_to(kv_ref[1, pl.ds(ci * gc + co, 1), :], sq)
            cf = pltpu.roll(vf, 0, 1, stride=1, stride_axis=0)
            cbw = pltpu.roll(vb, 0, 1, stride=1, stride_axis=0)
            m_ref[pl.ds(r0, CHUNK), co * CHUNK:(co + 1) * CHUNK] = jnp.where(upper, cf, cbw).astype(BF16)
        return carry

    lax.fori_loop(0, gc, build, 0)
    for ci in range(gc):
        w_re, w_im = _cmul(lt_re, lt_im, bb_re[ci:ci + 1], bb_im[ci:ci + 1])
        m_ref[ci * CHUNK:(ci + 1) * CHUNK, gc * CHUNK:] = jnp.concatenate([w_re, w_im], axis=1).astype(BF16)
        g_re, g_im = _cmul(et_re, et_im, c_re[ci:ci + 1], c_im[ci:ci + 1])
        cm_ref[ci * CHUNK:(ci + 1) * CHUNK, :] = jnp.concatenate([g_re, -g_im], axis=1).astype(BF16)

    x = jnp.concatenate([x_ref[ci].astype(BF16) for ci in range(gc)], axis=1)
    res = _dot(x, m_ref[...])
    y0 = res[:, :gc * CHUNK]
    s_ref[0] = res[:, gc * CHUNK:gc * CHUNK + ns]
    s_ref[1] = res[:, gc * CHUNK + ns:]

    is_fwd = lax.broadcasted_iota(jnp.int32, (nbatch, ns), 1) < S5_STATE

    def step(i, st):
        st_re, st_im = st
        rows_f = pl.ds(i, nbatch, stride=nch)
        rows_b = pl.ds(nch - 1 - i, nbatch, stride=nch)
        xf_ref[0, rows_f, :] = st_re
        xf_ref[1, rows_f, :] = st_im
        xb_ref[0, rows_b, :] = st_re
        xb_ref[1, rows_b, :] = st_im
        s_re = jnp.where(is_fwd, s_ref[0, rows_f, :], s_ref[0, rows_b, :])
        s_im = jnp.where(is_fwd, s_ref[1, rows_f, :], s_ref[1, rows_b, :])
        n_re, n_im = _cmul(a_re, a_im, st_re, st_im)
        return n_re + s_re, n_im + s_im

    zero = jnp.zeros((nbatch, ns), F32)
    lax.fori_loop(0, nch, step, (zero, zero))
    is_fwd_all = lax.broadcasted_iota(jnp.int32, (nrows, ns), 1) < S5_STATE
    xin = jnp.concatenate([jnp.where(is_fwd_all, xf_ref[0], xb_ref[0]),
                           jnp.where(is_fwd_all, xf_ref[1], xb_ref[1])], axis=1).astype(BF16)
    y = y0 + _dot_nt(xin, cm_ref[...])
    for co in range(gc):
        y_ref[co] = y[:, co * CHUNK:(co + 1) * CHUNK]


def _s5(xat3, tabs, nbatch):
    sw, nrows, _ = xat3.shape
    ngroups = sw // S5_GROUP
    gc, ns = S5_GROUP, 2 * S5_STATE

    def gspec(a):
        nd = a.ndim
        return pl.BlockSpec((1,) + a.shape[1:], lambda g: (g,) + (0,) * (nd - 1))

    return pl.pallas_call(
        functools.partial(_s5_kernel, nbatch=nbatch),
        grid=(ngroups,),
        in_specs=[pl.BlockSpec((gc, nrows, CHUNK), lambda g: (g, 0, 0))] + [gspec(a) for a in tabs],
        out_specs=pl.BlockSpec((gc, nrows, CHUNK), lambda g: (g, 0, 0)),
        out_shape=jax.ShapeDtypeStruct((sw, nrows, CHUNK), F32),
        scratch_shapes=[pltpu.VMEM((gc * CHUNK, gc * CHUNK + 2 * ns), BF16),
                        pltpu.VMEM((gc * CHUNK, 2 * ns), BF16),
                        pltpu.VMEM((2, gc * gc, CHUNK), F32),
                        pltpu.VMEM((2, nrows, ns), F32),
                        pltpu.VMEM((2, nrows, ns), F32),
                        pltpu.VMEM((2, nrows, ns), F32)],
        compiler_params=_cparams("arbitrary"),
        name="s5",
    )(xat3, *tabs)


def _s5_tables(lam_re, lam_im, log_step, b_re, b_im, c_re, c_im, d):
    ng, npar, nc = b_re.shape[1], b_re.shape[2], b_re.shape[3]
    lam_re, lam_im = lam_re.astype(F32), lam_im.astype(F32)
    step = jnp.exp(log_step.astype(F32))[..., None]
    mag = jnp.exp(lam_re * step)
    lb_re, lb_im = mag * jnp.cos(lam_im * step), mag * jnp.sin(lam_im * step)
    den = lam_re * lam_re + lam_im * lam_im
    q_re = ((lb_re - 1.0) * lam_re + lb_im * lam_im) / den
    q_im = (lb_im * lam_re - (lb_re - 1.0) * lam_im) / den
    lanes = lambda a: a.transpose(1, 0, 2).reshape(ng, 2 * npar)
    rows = jnp.stack([lanes(a) for a in (lb_re, lb_im, q_re, q_im)], axis=1)
    rows = jnp.pad(rows, ((0, 0), (0, 4), (0, 0)))
    cols = rows[:, :2].transpose(0, 2, 1)
    braw = jnp.stack([a.astype(F32).transpose(1, 3, 0, 2).reshape(ng, nc, 2 * npar)
                      for a in (b_re, b_im)], axis=1)
    cmat = jnp.stack([a.astype(F32).transpose(1, 2, 0, 3).reshape(ng, nc, 2 * npar)
                      for a in (c_re, c_im)], axis=1)
    eye = jnp.eye(nc, dtype=F32)
    dcol = (d.astype(F32).reshape(ng, nc, 1) * eye[None]).reshape(ng, nc * nc, 1)
    return rows, cols, braw, cmat, dcol


def _merge_kernel(yt_ref, ga_ref, pb_ref, x_ref, wglut_ref, wpa_ref, wout_ref, h_ref):
    def body(r0, sub):
        rows = pl.ds(r0, TOKEN_SUB)
        ncs = TOKEN_SUB // CHUNK
        y = jax.nn.gelu(jnp.concatenate([yt_ref[:, sub * ncs + c, :] for c in range(ncs)], axis=1))
        gl = _dot(wglut_ref[...], y.astype(BF16))
        ya = (y * jax.nn.sigmoid(gl)).T.astype(BF16)
        merged = ga_ref[rows, :].astype(F32) * _dot(ya, wpa_ref[...]) + pb_ref[rows, :].astype(F32)
        h_ref[rows, :] = x_ref[rows, :] + _dot(merged.astype(BF16), wout_ref[...])

    _sub_loop(x_ref.shape[0], body)


def _merge(yt3, ga, pb, x2, wglut, wpa, wout):
    t, d = x2.shape
    sw = yt3.shape[0]
    tm = TOKEN_BLOCK
    return pl.pallas_call(
        _merge_kernel,
        grid=(t // tm,),
        in_specs=[pl.BlockSpec((sw, tm // CHUNK, CHUNK), lambda i: (0, i, 0)),
                  pl.BlockSpec((tm, d), lambda i: (i, 0)),
                  pl.BlockSpec((tm, d), lambda i: (i, 0)),
                  pl.BlockSpec((tm, d), lambda i: (i, 0)),
                  _const_spec(wglut.shape), _const_spec(wpa.shape), _const_spec(wout.shape)],
        out_specs=pl.BlockSpec((tm, d), lambda i: (i, 0)),
        out_shape=jax.ShapeDtypeStruct((t, d), F32),
        compiler_params=_cparams("arbitrary"),
        name="merge",
    )(yt3, ga, pb, x2, wglut, wpa, wout)


def _xattn_kernel(h_ref, g_ref, wq_ref, k_ref, v_ref, wo_ref, o_ref):
    h = h_ref[0]
    hn = _rms(h, g_ref[...]).astype(BF16)
    q = _dot(hn, wq_ref[...]).astype(BF16)
    hd = q.shape[1] // XATTN_HEADS
    outs = []
    for hh in range(XATTN_HEADS):
        sl = slice(hh * hd, (hh + 1) * hd)
        s = _dot_nt(q[:, sl], k_ref[0, :, sl])
        e = jnp.exp(s - jnp.max(s, axis=-1, keepdims=True))
        p = e / jnp.sum(e, axis=-1, keepdims=True)
        outs.append(_dot(p.astype(BF16), v_ref[0, :, sl]))
    o = jnp.concatenate(outs, axis=1).astype(BF16)
    o_ref[0] = h + _dot(o, wo_ref[...])


def _xattn(h3, g, wq, k, v, wo, tm):
    b, l, d = h3.shape
    m = k.shape[1]
    return pl.pallas_call(
        _xattn_kernel,
        grid=(b, l // tm),
        in_specs=[pl.BlockSpec((1, tm, d), lambda i, j: (i, j, 0)),
                  _const_spec(g.shape), _const_spec(wq.shape),
                  pl.BlockSpec((1, m, d), lambda i, j: (i, 0, 0)),
                  pl.BlockSpec((1, m, d), lambda i, j: (i, 0, 0)),
                  _const_spec(wo.shape)],
        out_specs=pl.BlockSpec((1, tm, d), lambda i, j: (i, j, 0)),
        out_shape=jax.ShapeDtypeStruct((b, l, d), F32),
        compiler_params=_cparams("arbitrary", "arbitrary"),
        name="xattn",
    )(h3, g, wq, k, v, wo)


def _ffn_kernel(h_ref, g_ref, wg_ref, wu_ref, wd_ref, gf_ref, o_ref):
    h = h_ref[...]
    hn = _rms(h, g_ref[...]).astype(BF16)
    act = (jax.nn.silu(_dot(hn, wg_ref[...])) * _dot(hn, wu_ref[...])).astype(BF16)
    o_ref[...] = _rms(h + _dot(act, wd_ref[...]), gf_ref[...])


def _ffn(h2, g, wg, wu, wd, gf, tm):
    t, d = h2.shape
    consts = (g, wg, wu, wd, gf)
    return pl.pallas_call(
        _ffn_kernel,
        grid=(t // tm,),
        in_specs=[pl.BlockSpec((tm, d), lambda i: (i, 0))] + [_const_spec(c.shape) for c in consts],
        out_specs=pl.BlockSpec((tm, d), lambda i: (i, 0)),
        out_shape=jax.ShapeDtypeStruct((t, d), F32),
        compiler_params=_cparams("arbitrary"),
        name="ffn",
    )(h2, *consts)


def _layer(h, kmem, vmem, mix_g, w_in, s5p, w_glu, ln_g, ln_b, sgu_w, sgu_bias, wpa, wpb, wout,
           xg, wq, wxo, fg, wgate, wup, wdown, final_g):
    b, l, d = h.shape
    t = b * l
    sw = w_glu.shape[0]
    su = ln_g.shape[0]
    off_u, off_v, off_ga, off_gb = sw, sw + su, sw + 2 * su, sw + 2 * su + d
    row = lambda a: a.astype(F32).reshape(1, -1)
    x2 = h.reshape(t, d)
    bias_full = jnp.repeat(sgu_bias.astype(F32).T, su // SGU_HEADS, axis=1)
    xat3, ga, pb = _in_proj(
        x2, row(mix_g), w_in[:, :off_u].T.astype(BF16), w_in[:, off_u:off_v].astype(BF16),
        w_in[:, off_v:off_ga].astype(BF16), w_in[:, off_ga:off_gb].astype(BF16),
        w_in[:, off_gb:].astype(BF16), row(ln_g), row(ln_b), sgu_w.astype(BF16),
        bias_full, wpb.astype(BF16))
    yt3 = _s5(xat3, _s5_tables(*s5p), b)
    h1 = _merge(yt3, ga, pb, x2, w_glu.T.astype(BF16), wpa.astype(BF16), wout.astype(BF16))
    h2 = _xattn(h1.reshape(b, l, d), row(xg), wq.astype(BF16), kmem, vmem, wxo.astype(BF16), 256)
    out = _ffn(h2.reshape(t, d), row(fg), wgate.astype(BF16), wup.astype(BF16), wdown.astype(BF16),
               row(final_g), 256)
    return out.reshape(b, l, d)


def kernel(x, mem, mix_norm_g, w_in, s5_lam_re, s5_lam_im, s5_log_step, s5_b_re, s5_b_im, s5_c_re, s5_c_im, s5_d, s5_w_glu, sgu_ln_g, sgu_ln_b, sgu_w, sgu_bias, w_proj_a, w_proj_b, w_out, xattn_norm_g, mem_norm_g, w_q, w_k, w_v, w_xo, ffn_norm_g, w_gate, w_up, w_down, final_norm_g):
    depth = w_in.shape[0]
    assert depth == 1, "the final rms_norm is fused into the last layer's ffn kernel"
    hd = x.shape[-1] // XATTN_HEADS
    i = 0
    kmem, vmem = _mem_kv(mem, mem_norm_g.astype(F32).reshape(1, -1), w_k[i].astype(BF16),
                         w_v[i].astype(BF16), hd ** -0.5)
    s5p = (s5_lam_re[i], s5_lam_im[i], s5_log_step[i], s5_b_re[i], s5_b_im[i], s5_c_re[i],
           s5_c_im[i], s5_d[i])
    return _layer(x, kmem, vmem, mix_norm_g[i], w_in[i], s5p, s5_w_glu[i], sgu_ln_g[i], sgu_ln_b[i],
                  sgu_w[i], sgu_bias[i], w_proj_a[i], w_proj_b[i], w_out[i], xattn_norm_g[i], w_q[i],
                  w_xo[i], ffn_norm_g[i], w_gate[i], w_up[i], w_down[i], final_norm_g)
```

```python
import functools
from typing import NamedTuple

import jax
import jax.numpy as jnp
from jax import lax
from jax.experimental import pallas as pl
from jax.experimental.pallas import tpu as pltpu

F32 = jnp.float32
BF16 = jnp.bfloat16

EPS = 1e-6
LANES = 128
CHUNK = 128
S5_GROUP = 16
S5_STATE = 64
SGU_HEADS = 8
XATTN_HEADS = 4
VMEM_LIMIT = 56 * 1024 * 1024
TOKEN_BLOCK = 1024
TOKEN_SUB = 256


def _cparams(*sem):
    return pltpu.CompilerParams(dimension_semantics=sem, vmem_limit_bytes=VMEM_LIMIT)


def _const_spec(shape):
    nd = len(shape)
    return pl.BlockSpec(shape, lambda *_: (0,) * nd, pipeline_mode=pl.Buffered(1))


def _rms(x, g):
    return x * lax.rsqrt(jnp.mean(x * x, axis=-1, keepdims=True) + EPS) * g


def _dot(a, b):
    return jnp.dot(a, b, preferred_element_type=F32)


def _dot_nt(a, b):
    return lax.dot_general(a, b, (((1,), (1,)), ((), ())), preferred_element_type=F32)


def _sub_loop(n_tokens, body):
    def step(sub, carry):
        body(pl.multiple_of(sub * TOKEN_SUB, TOKEN_SUB), sub)
        return carry
    lax.fori_loop(0, n_tokens // TOKEN_SUB, step, 0, unroll=2)


def _mem_kv_kernel(mem_ref, g_ref, wk_ref, wv_ref, k_ref, v_ref, *, scale):
    mn = _rms(mem_ref[0], g_ref[...]).astype(BF16)
    k_ref[0] = (_dot(mn, wk_ref[...]) * scale).astype(BF16)
    v_ref[0] = _dot(mn, wv_ref[...]).astype(BF16)


def _mem_kv(mem, g, wk, wv, scale):
    b, m, d = mem.shape
    return pl.pallas_call(
        functools.partial(_mem_kv_kernel, scale=scale),
        grid=(b,),
        in_specs=[pl.BlockSpec((1, m, d), lambda i: (i, 0, 0)),
                  _const_spec((1, d)), _const_spec((d, d)), _const_spec((d, d))],
        out_specs=[pl.BlockSpec((1, m, d), lambda i: (i, 0, 0))] * 2,
        out_shape=[jax.ShapeDtypeStruct((b, m, d), BF16)] * 2,
        compiler_params=_cparams("arbitrary"),
        name="mem_kv",
    )(mem, g, wk, wv)


def _in_proj_kernel(x_ref, g_ref, wat_ref, wu_ref, wv_ref, wga_ref, wgb_ref, lng_ref, lnb_ref,
                    ws_ref, bias_ref, wpb_ref, xat_ref, ga_ref, pb_ref):
    def body(r0, sub):
        rows = pl.ds(r0, TOKEN_SUB)
        n = _rms(x_ref[rows, :], g_ref[...]).astype(BF16)
        xat = _dot_nt(wat_ref[...], n)
        for c in range(TOKEN_SUB // CHUNK):
            xat_ref[:, sub * (TOKEN_SUB // CHUNK) + c, :] = xat[:, c * CHUNK:(c + 1) * CHUNK]
        ga_ref[rows, :] = jax.nn.sigmoid(_dot(n, wga_ref[...])).astype(BF16)
        zv = jax.nn.gelu(_dot(n, wv_ref[...]))
        zc = zv - jnp.mean(zv, axis=-1, keepdims=True)
        zvn = (zc * lax.rsqrt(jnp.mean(zc * zc, axis=-1, keepdims=True) + EPS) * lng_ref[...]
               + lnb_ref[...]).astype(BF16)
        hd = zvn.shape[1] // SGU_HEADS
        chunks = []
        for c in range(TOKEN_SUB // CHUNK):
            blk = zvn[c * CHUNK:(c + 1) * CHUNK]
            chunks.append(jnp.concatenate(
                [_dot(ws_ref[h], blk[:, h * hd:(h + 1) * hd]) for h in range(SGU_HEADS)], axis=1)
                + bias_ref[...])
        sv = jnp.concatenate(chunks, axis=0)
        zu = jax.nn.gelu(_dot(n, wu_ref[...]))
        yb = (zu * sv).astype(BF16)
        gb = jax.nn.sigmoid(_dot(n, wgb_ref[...]))
        pb_ref[rows, :] = (gb * _dot(yb, wpb_ref[...])).astype(BF16)

    _sub_loop(x_ref.shape[0], body)


def _in_proj(x2, g, wat, wu, wv, wga, wgb, lng, lnb, ws, bias_full, wpb):
    t, d = x2.shape
    sw = wat.shape[0]
    tm = TOKEN_BLOCK
    consts = (g, wat, wu, wv, wga, wgb, lng, lnb, ws, bias_full, wpb)
    return pl.pallas_call(
        _in_proj_kernel,
        grid=(t // tm,),
        in_specs=[pl.BlockSpec((tm, d), lambda i: (i, 0))] + [_const_spec(c.shape) for c in consts],
        out_specs=[pl.BlockSpec((sw, tm // CHUNK, CHUNK), lambda i: (0, i, 0)),
                   pl.BlockSpec((tm, d), lambda i: (i, 0)),
                   pl.BlockSpec((tm, d), lambda i: (i, 0))],
        out_shape=[jax.ShapeDtypeStruct((sw, t // CHUNK, CHUNK), F32),
                   jax.ShapeDtypeStruct((t, d), BF16),
                   jax.ShapeDtypeStruct((t, d), BF16)],
        compiler_params=_cparams("arbitrary"),
        name="in_proj",
    )(x2, *consts)


def _cmul(ar, ai, br, bi):
    return ar * br - ai * bi, ar * bi + ai * br


def _cpow(br, bi, e, nbits):
    rr = ri = None
    for k in range(nbits):
        bit = ((e >> k) & 1) == 1
        sr, si = jnp.where(bit, br, 1.0), jnp.where(bit, bi, 0.0)
        rr, ri = (sr, si) if rr is None else _cmul(rr, ri, sr, si)
        if k + 1 < nbits:
            br, bi = _cmul(br, bi, br, bi)
    return rr, ri


class _S5Tabs(NamedTuple):
    pk_f: jax.Array
    pk_b: jax.Array
    lt: tuple
    et: tuple
    bb: tuple
    cc: tuple
    a: tuple


def _s5_tables_in_kernel(row_ref, col_ref, bc_ref):
    gc, half, ns = S5_GROUP, S5_GROUP // 2, 2 * S5_STATE
    sq = (CHUNK, CHUNK)
    sub = lax.broadcasted_iota(jnp.int32, sq, 0)
    lan = lax.broadcasted_iota(jnp.int32, sq, 1)
    lb_re, lb_im = row_ref[0, 0:1, :], row_ref[0, 1:2, :]
    q_re, q_im = row_ref[0, 2:3, :], row_ref[0, 3:4, :]
    fwd_lane = lan < S5_STATE
    lt = _cpow(lb_re, lb_im, jnp.where(fwd_lane, CHUNK - 1 - sub, sub), 7)
    et = _cpow(lb_re, lb_im, jnp.where(fwd_lane, sub + 1, CHUNK - sub), 8)
    p_re, p_im = _cpow(col_ref[0, :, 0:1], col_ref[0, :, 1:2],
                       jnp.where(sub < S5_STATE, lan, (CHUNK - lan) & (CHUNK - 1)), 7)
    a_re, a_im = lb_re, lb_im
    for _ in range(7):
        a_re, a_im = _cmul(a_re, a_im, a_re, a_im)
    bb = _cmul(q_re, q_im, bc_ref[0, 0], bc_ref[0, 1])
    c_re, c_im = bc_ref[0, 2], bc_ref[0, 3]

    rhs_f = jnp.concatenate([p_re[:S5_STATE], p_im[:S5_STATE]], axis=0)
    rhs_b = jnp.concatenate([p_re[S5_STATE:], p_im[S5_STATE:]], axis=0)
    lane_f = lax.broadcasted_iota(jnp.int32, (gc * half, ns), 1) < S5_STATE
    lane_tap = lax.broadcasted_iota(jnp.int32, (gc * half, CHUNK), 1)
    taps = []
    for par in range(2):
        cpr, cpi = c_re[par * half:(par + 1) * half], c_im[par * half:(par + 1) * half]
        cb = [_cmul(cpr, cpi, bb[0][ci:ci + 1], bb[1][ci:ci + 1]) for ci in range(gc)]
        cb_re = jnp.concatenate([v[0] for v in cb], axis=0)
        cb_im = jnp.concatenate([v[1] for v in cb], axis=0)
        lhs_f = jnp.where(lane_f, cb_re, -pltpu.roll(cb_im, S5_STATE, 1))
        lhs_b = jnp.where(lane_f, pltpu.roll(cb_re, S5_STATE, 1), -cb_im)
        kf = jnp.dot(lhs_f, rhs_f, preferred_element_type=F32, precision=lax.Precision.HIGHEST)
        kb = jnp.dot(lhs_b, rhs_b, preferred_element_type=F32, precision=lax.Precision.HIGHEST)
        kf = kf + jnp.where(lane_tap == 0, kb[:, 0:1] + col_ref[0, :, 2 + par:3 + par], 0.0)
        taps.append((kf, kb))

    def bf16_bits(v):
        u = lax.bitcast_convert_type(v, jnp.uint32)
        return (u + jnp.uint32(0x7FFF) + ((u >> 16) & jnp.uint32(1))) & jnp.uint32(0xFFFF0000)

    def pack(even, odd):
        return lax.bitcast_convert_type(bf16_bits(even) | (bf16_bits(odd) >> 16), jnp.int32)

    return _S5Tabs(pack(taps[0][0], taps[1][0]), pack(taps[0][1], taps[1][1]),
                   lt, et, bb, (c_re, c_im), (a_re, a_im))


def _s5_build_small(tabs, slot, bm_ref, cm_ref, a_ref):
    half = S5_GROUP // 2
    for ch in range(S5_GROUP):
        w_re, w_im = _cmul(*tabs.lt, tabs.bb[0][ch:ch + 1], tabs.bb[1][ch:ch + 1])
        bm_ref[slot, ch * CHUNK:(ch + 1) * CHUNK, :] = jnp.concatenate([w_re, w_im], axis=1).astype(BF16)
        r = (ch % 2) * half + ch // 2
        g_re, g_im = _cmul(*tabs.et, tabs.cc[0][r:r + 1], tabs.cc[1][r:r + 1])
        cm_ref[slot, ch * CHUNK:(ch + 1) * CHUNK, :] = jnp.concatenate([g_re, -g_im], axis=1).astype(BF16)
    a_ref[slot, 0] = jnp.broadcast_to(tabs.a[0], a_ref.shape[2:])
    a_ref[slot, 1] = jnp.broadcast_to(tabs.a[1], a_ref.shape[2:])


def _s5_build_pair(tabs, slot, q, m_ref):
    half = S5_GROUP // 2
    sq = (CHUNK, CHUNK)
    upper = lax.broadcasted_iota(jnp.int32, sq, 1) >= lax.broadcasted_iota(jnp.int32, sq, 0)
    for ci in range(S5_GROUP):
        r = ci * half + q
        vf = jnp.broadcast_to(tabs.pk_f[r:r + 1], sq)
        vb = jnp.broadcast_to(tabs.pk_b[r:r + 1], sq)
        cf = pltpu.roll(vf, 0, 1, stride=1, stride_axis=0)
        cbw = pltpu.roll(vb, 0, 1, stride=1, stride_axis=0)
        w = lax.bitcast_convert_type(jnp.where(upper, cf, cbw), jnp.uint32)
        even = lax.bitcast_convert_type(w & jnp.uint32(0xFFFF0000), F32)
        odd = lax.bitcast_convert_type(w << 16, F32)
        m_ref[slot, ci * CHUNK:(ci + 1) * CHUNK, 2 * q * CHUNK:(2 * q + 2) * CHUNK] = (
            jnp.concatenate([even, odd], axis=1).astype(BF16))


def _s5_apply(x_ref, y_ref, gi, slot, m_ref, bm_ref, cm_ref, a_ref, s_ref, xf_ref, xb_ref, nbatch,
              between):
    gc, ns = S5_GROUP, 2 * S5_STATE
    nrows = x_ref.shape[1]
    nch = nrows // nbatch
    x = jnp.concatenate([x_ref[gi * gc + ci].astype(BF16) for ci in range(gc)], axis=1)
    s = _dot(x, bm_ref[slot])
    s_ref[0] = s[:, :ns]
    s_ref[1] = s[:, ns:]
    a_re, a_im = a_ref[slot, 0, 0:1, :], a_ref[slot, 1, 0:1, :]
    is_fwd = lax.broadcasted_iota(jnp.int32, (nbatch, ns), 1) < S5_STATE
    st_re = st_im = jnp.zeros((nbatch, ns), F32)
    for i in range(nch):
        rows_f = pl.ds(i, nbatch, stride=nch)
        rows_b = pl.ds(nch - 1 - i, nbatch, stride=nch)
        xf_ref[0, rows_f, :] = st_re
        xf_ref[1, rows_f, :] = st_im
        xb_ref[0, rows_b, :] = st_re
        xb_ref[1, rows_b, :] = st_im
        if i + 1 < nch:
            s_re = jnp.where(is_fwd, s_ref[0, rows_f, :], s_ref[0, rows_b, :])
            s_im = jnp.where(is_fwd, s_ref[1, rows_f, :], s_ref[1, rows_b, :])
            n_re, n_im = _cmul(a_re, a_im, st_re, st_im)
            st_re, st_im = n_re + s_re, n_im + s_im
    is_fwd_all = lax.broadcasted_iota(jnp.int32, (nrows, ns), 1) < S5_STATE
    xin = jnp.concatenate([jnp.where(is_fwd_all, xf_ref[0], xb_ref[0]),
                           jnp.where(is_fwd_all, xf_ref[1], xb_ref[1])], axis=1).astype(BF16)
    for q in range(gc // 2):
        between(q)
        cols = slice(2 * q * CHUNK, (2 * q + 2) * CHUNK)
        y = _dot(x, m_ref[slot, :, cols]) + _dot_nt(xin, cm_ref[slot, cols, :])
        y_ref[gi * gc + 2 * q] = y[:, :CHUNK]
        y_ref[gi * gc + 2 * q + 1] = y[:, CHUNK:]


def _s5_kernel(x_ref, row0, col0, bc0, row1, col1, bc1,
               y_ref, m_ref, bm_ref, cm_ref, a_ref, s_ref, xf_ref, xb_ref, *, nbatch):
    scr = (m_ref, bm_ref, cm_ref, a_ref, s_ref, xf_ref, xb_ref)
    k = pl.program_id(0)

    @pl.when(k == 0)
    def _():
        tabs = _s5_tables_in_kernel(row0, col0, bc0)
        _s5_build_small(tabs, 0, bm_ref, cm_ref, a_ref)
        for q in range(S5_GROUP // 2):
            _s5_build_pair(tabs, 0, q, m_ref)

    for cur in range(2):
        @pl.when(k % 2 == cur)
        def _():
            tabs = _s5_tables_in_kernel(row1, col1, bc1)
            _s5_build_small(tabs, 1 - cur, bm_ref, cm_ref, a_ref)
            _s5_apply(x_ref, y_ref, 0, cur, *scr, nbatch,
                      functools.partial(_s5_build_pair, tabs, 1 - cur, m_ref=m_ref))


def _s5(xat3, tabs, nbatch):
    sw, nrows, _ = xat3.shape
    gc, ns = S5_GROUP, 2 * S5_STATE
    nsteps = sw // gc
    last = nsteps - 1

    def tspecs(index):
        return [pl.BlockSpec((1,) + a.shape[1:], lambda k, n=a.ndim: (index(k),) + (0,) * (n - 1))
                for a in tabs]

    return pl.pallas_call(
        functools.partial(_s5_kernel, nbatch=nbatch),
        grid=(nsteps,),
        in_specs=[pl.BlockSpec((gc, nrows, CHUNK), lambda k: (k, 0, 0))]
                 + tspecs(lambda k: 0) + tspecs(lambda k: jnp.minimum(k + 1, last)),
        out_specs=pl.BlockSpec((gc, nrows, CHUNK), lambda k: (k, 0, 0)),
        out_shape=jax.ShapeDtypeStruct((sw, nrows, CHUNK), F32),
        scratch_shapes=[pltpu.VMEM((2, gc * CHUNK, gc * CHUNK), BF16),
                        pltpu.VMEM((2, gc * CHUNK, 2 * ns), BF16),
                        pltpu.VMEM((2, gc * CHUNK, 2 * ns), BF16),
                        pltpu.VMEM((2, 2, 8, ns), F32),
                        pltpu.VMEM((2, nrows, ns), F32),
                        pltpu.VMEM((2, nrows, ns), F32),
                        pltpu.VMEM((2, nrows, ns), F32)],
        compiler_params=_cparams("arbitrary"),
        name="s5",
    )(xat3, *tabs, *tabs)


def _s5_tables(lam_re, lam_im, log_step, b_re, b_im, c_re, c_im, d):
    ng, npar, nc = b_re.shape[1], b_re.shape[2], b_re.shape[3]
    half = nc // 2
    lam_re, lam_im = lam_re.astype(F32), lam_im.astype(F32)
    step = jnp.exp(log_step.astype(F32))[..., None]
    mag = jnp.exp(lam_re * step)
    lb_re, lb_im = mag * jnp.cos(lam_im * step), mag * jnp.sin(lam_im * step)
    den = lam_re * lam_re + lam_im * lam_im
    q_re = ((lb_re - 1.0) * lam_re + lb_im * lam_im) / den
    q_im = (lb_im * lam_re - (lb_re - 1.0) * lam_im) / den
    lanes = lambda a: a.transpose(1, 0, 2).reshape(ng, 2 * npar)
    rows = jnp.stack([lanes(a) for a in (lb_re, lb_im, q_re, q_im)], axis=1)
    rows = jnp.pad(rows, ((0, 0), (0, 4), (0, 0)))
    ci = jnp.arange(nc)[:, None, None]
    co = 2 * jnp.arange(half)[None, :, None] + jnp.arange(2)[None, None, :]
    dsel = (ci == co).astype(F32).reshape(nc * half, 2)
    dcol = jnp.repeat(d.astype(F32).reshape(ng, nc), half, axis=1)[..., None] * dsel[None]
    cols = jnp.concatenate([rows[:, :2].transpose(0, 2, 1), dcol], axis=2)
    perm = jnp.concatenate([jnp.arange(0, nc, 2), jnp.arange(1, nc, 2)])
    bc = jnp.stack([b_re.astype(F32).transpose(1, 3, 0, 2).reshape(ng, nc, 2 * npar),
                    b_im.astype(F32).transpose(1, 3, 0, 2).reshape(ng, nc, 2 * npar),
                    c_re.astype(F32).transpose(1, 2, 0, 3).reshape(ng, nc, 2 * npar)[:, perm],
                    c_im.astype(F32).transpose(1, 2, 0, 3).reshape(ng, nc, 2 * npar)[:, perm]],
                   axis=1)
    return rows, cols, bc


def _merge_kernel(yt_ref, ga_ref, pb_ref, x_ref, wglut_ref, wpa_ref, wout_ref, h_ref):
    def body(r0, sub):
        rows = pl.ds(r0, TOKEN_SUB)
        ncs = TOKEN_SUB // CHUNK
        y = jax.nn.gelu(jnp.concatenate([yt_ref[:, sub * ncs + c, :] for c in range(ncs)], axis=1))
        gl = _dot(wglut_ref[...], y.astype(BF16))
        ya = (y * jax.nn.sigmoid(gl)).T.astype(BF16)
        merged = ga_ref[rows, :].astype(F32) * _dot(ya, wpa_ref[...]) + pb_ref[rows, :].astype(F32)
        h_ref[rows, :] = x_ref[rows, :] + _dot(merged.astype(BF16), wout_ref[...])

    _sub_loop(x_ref.shape[0], body)


def _merge(yt3, ga, pb, x2, wglut, wpa, wout):
    t, d = x2.shape
    sw = yt3.shape[0]
    tm = TOKEN_BLOCK
    return pl.pallas_call(
        _merge_kernel,
        grid=(t // tm,),
        in_specs=[pl.BlockSpec((sw, tm // CHUNK, CHUNK), lambda i: (0, i, 0)),
                  pl.BlockSpec((tm, d), lambda i: (i, 0)),
                  pl.BlockSpec((tm, d), lambda i: (i, 0)),
                  pl.BlockSpec((tm, d), lambda i: (i, 0)),
                  _const_spec(wglut.shape), _const_spec(wpa.shape), _const_spec(wout.shape)],
        out_specs=pl.BlockSpec((tm, d), lambda i: (i, 0)),
        out_shape=jax.ShapeDtypeStruct((t, d), F32),
        compiler_params=_cparams("arbitrary"),
        name="merge",
    )(yt3, ga, pb, x2, wglut, wpa, wout)


def _xattn_kernel(h_ref, g_ref, wq_ref, k_ref, v_ref, wo_ref, o_ref):
    h = h_ref[0]
    hn = _rms(h, g_ref[...]).astype(BF16)
    q = _dot(hn, wq_ref[...]).astype(BF16)
    hd = q.shape[1] // XATTN_HEADS
    outs = []
    for hh in range(XATTN_HEADS):
        sl = slice(hh * hd, (hh + 1) * hd)
        s = _dot_nt(q[:, sl], k_ref[0, :, sl])
        e = jnp.exp(s - jnp.max(s, axis=-1, keepdims=True))
        p = e / jnp.sum(e, axis=-1, keepdims=True)
        outs.append(_dot(p.astype(BF16), v_ref[0, :, sl]))
    o = jnp.concatenate(outs, axis=1).astype(BF16)
    o_ref[0] = h + _dot(o, wo_ref[...])


def _xattn(h3, g, wq, k, v, wo, tm):
    b, l, d = h3.shape
    m = k.shape[1]
    return pl.pallas_call(
        _xattn_kernel,
        grid=(b, l // tm),
        in_specs=[pl.BlockSpec((1, tm, d), lambda i, j: (i, j, 0)),
                  _const_spec(g.shape), _const_spec(wq.shape),
                  pl.BlockSpec((1, m, d), lambda i, j: (i, 0, 0)),
                  pl.BlockSpec((1, m, d), lambda i, j: (i, 0, 0)),
                  _const_spec(wo.shape)],
        out_specs=pl.BlockSpec((1, tm, d), lambda i, j: (i, j, 0)),
        out_shape=jax.ShapeDtypeStruct((b, l, d), F32),
        compiler_params=_cparams("arbitrary", "arbitrary"),
        name="xattn",
    )(h3, g, wq, k, v, wo)


def _ffn_kernel(h_ref, g_ref, wg_ref, wu_ref, wd_ref, gf_ref, o_ref):
    h = h_ref[...]
    hn = _rms(h, g_ref[...]).astype(BF16)
    act = (jax.nn.silu(_dot(hn, wg_ref[...])) * _dot(hn, wu_ref[...])).astype(BF16)
    o_ref[...] = _rms(h + _dot(act, wd_ref[...]), gf_ref[...])


def _ffn(h2, g, wg, wu, wd, gf, tm):
    t, d = h2.shape
    consts = (g, wg, wu, wd, gf)
    return pl.pallas_call(
        _ffn_kernel,
        grid=(t // tm,),
        in_specs=[pl.BlockSpec((tm, d), lambda i: (i, 0))] + [_const_spec(c.shape) for c in consts],
        out_specs=pl.BlockSpec((tm, d), lambda i: (i, 0)),
        out_shape=jax.ShapeDtypeStruct((t, d), F32),
        compiler_params=_cparams("arbitrary"),
        name="ffn",
    )(h2, *consts)


def _layer(h, kmem, vmem, mix_g, w_in, s5p, w_glu, ln_g, ln_b, sgu_w, sgu_bias, wpa, wpb, wout,
           xg, wq, wxo, fg, wgate, wup, wdown, final_g):
    b, l, d = h.shape
    t = b * l
    sw = w_glu.shape[0]
    su = ln_g.shape[0]
    off_u, off_v, off_ga, off_gb = sw, sw + su, sw + 2 * su, sw + 2 * su + d
    row = lambda a: a.astype(F32).reshape(1, -1)
    x2 = h.reshape(t, d)
    bias_full = jnp.repeat(sgu_bias.astype(F32).T, su // SGU_HEADS, axis=1)
    xat3, ga, pb = _in_proj(
        x2, row(mix_g), w_in[:, :off_u].T.astype(BF16), w_in[:, off_u:off_v].astype(BF16),
        w_in[:, off_v:off_ga].astype(BF16), w_in[:, off_ga:off_gb].astype(BF16),
        w_in[:, off_gb:].astype(BF16), row(ln_g), row(ln_b), sgu_w.astype(BF16),
        bias_full, wpb.astype(BF16))
    yt3 = _s5(xat3, _s5_tables(*s5p), b)
    h1 = _merge(yt3, ga, pb, x2, w_glu.T.astype(BF16), wpa.astype(BF16), wout.astype(BF16))
    h2 = _xattn(h1.reshape(b, l, d), row(xg), wq.astype(BF16), kmem, vmem, wxo.astype(BF16), 256)
    out = _ffn(h2.reshape(t, d), row(fg), wgate.astype(BF16), wup.astype(BF16), wdown.astype(BF16),
               row(final_g), 256)
    return out.reshape(b, l, d)


def kernel(x, mem, mix_norm_g, w_in, s5_lam_re, s5_lam_im, s5_log_step, s5_b_re, s5_b_im, s5_c_re, s5_c_im, s5_d, s5_w_glu, sgu_ln_g, sgu_ln_b, sgu_w, sgu_bias, w_proj_a, w_proj_b, w_out, xattn_norm_g, mem_norm_g, w_q, w_k, w_v, w_xo, ffn_norm_g, w_gate, w_up, w_down, final_norm_g):
    depth = w_in.shape[0]
    assert depth == 1, "the final rms_norm is fused into the last layer's ffn kernel"
    hd = x.shape[-1] // XATTN_HEADS
    i = 0
    kmem, vmem = _mem_kv(mem, mem_norm_g.astype(F32).reshape(1, -1), w_k[i].astype(BF16),
                         w_v[i].astype(BF16), hd ** -0.5)
    s5p = (s5_lam_re[i], s5_lam_im[i], s5_log_step[i], s5_b_re[i], s5_b_im[i], s5_c_re[i],
           s5_c_im[i], s5_d[i])
    return _layer(x, kmem, vmem, mix_norm_g[i], w_in[i], s5p, s5_w_glu[i], sgu_ln_g[i], sgu_ln_b[i],
                  sgu_w[i], sgu_bias[i], w_proj_a[i], w_proj_b[i], w_out[i], xattn_norm_g[i], w_q[i],
                  w_xo[i], ffn_norm_g[i], w_gate[i], w_up[i], w_down[i], final_norm_g)
```

```python
import functools
from typing import NamedTuple

import jax
import jax.numpy as jnp
from jax import lax
from jax.experimental import pallas as pl
from jax.experimental.pallas import tpu as pltpu

F32 = jnp.float32
BF16 = jnp.bfloat16

EPS = 1e-6
LANES = 128
CHUNK = 128
S5_GROUP = 16
S5_STATE = 64
SGU_HEADS = 8
XATTN_HEADS = 4
VMEM_LIMIT = 56 * 1024 * 1024
TOKEN_BLOCK = 1024
TOKEN_SUB = 256


def _cparams(*sem):
    return pltpu.CompilerParams(dimension_semantics=sem, vmem_limit_bytes=VMEM_LIMIT)


def _const_spec(shape):
    nd = len(shape)
    return pl.BlockSpec(shape, lambda *_: (0,) * nd, pipeline_mode=pl.Buffered(1))


def _rms(x, g):
    return x * lax.rsqrt(jnp.mean(x * x, axis=-1, keepdims=True) + EPS) * g


def _dot(a, b):
    return jnp.dot(a, b, preferred_element_type=F32)


def _dot_nt(a, b):
    return lax.dot_general(a, b, (((1,), (1,)), ((), ())), preferred_element_type=F32)


def _sub_loop(n_tokens, body):
    def step(sub, carry):
        body(pl.multiple_of(sub * TOKEN_SUB, TOKEN_SUB), sub)
        return carry
    lax.fori_loop(0, n_tokens // TOKEN_SUB, step, 0, unroll=2)


def _mem_kv_kernel(mem_ref, g_ref, wk_ref, wv_ref, k_ref, v_ref, *, scale):
    mn = _rms(mem_ref[0], g_ref[...]).astype(BF16)
    k_ref[0] = (_dot(mn, wk_ref[...]) * scale).astype(BF16)
    v_ref[0] = _dot(mn, wv_ref[...]).astype(BF16)


def _mem_kv(mem, g, wk, wv, scale):
    b, m, d = mem.shape
    return pl.pallas_call(
        functools.partial(_mem_kv_kernel, scale=scale),
        grid=(b,),
        in_specs=[pl.BlockSpec((1, m, d), lambda i: (i, 0, 0)),
                  _const_spec((1, d)), _const_spec((d, d)), _const_spec((d, d))],
        out_specs=[pl.BlockSpec((1, m, d), lambda i: (i, 0, 0))] * 2,
        out_shape=[jax.ShapeDtypeStruct((b, m, d), BF16)] * 2,
        compiler_params=_cparams("arbitrary"),
        name="mem_kv",
    )(mem, g, wk, wv)


def _in_proj_kernel(x_ref, g_ref, wat_ref, wu_ref, wv_ref, wga_ref, wgb_ref, lng_ref, lnb_ref,
                    ws_ref, bias_ref, wpb_ref, xat_ref, ga_ref, pb_ref):
    def body(r0, sub):
        rows = pl.ds(r0, TOKEN_SUB)
        n = _rms(x_ref[rows, :], g_ref[...]).astype(BF16)
        xat = _dot_nt(wat_ref[...], n)
        for c in range(TOKEN_SUB // CHUNK):
            xat_ref[:, sub * (TOKEN_SUB // CHUNK) + c, :] = xat[:, c * CHUNK:(c + 1) * CHUNK]
        ga_ref[rows, :] = jax.nn.sigmoid(_dot(n, wga_ref[...])).astype(BF16)
        zv = jax.nn.gelu(_dot(n, wv_ref[...]))
        zc = zv - jnp.mean(zv, axis=-1, keepdims=True)
        zvn = (zc * lax.rsqrt(jnp.mean(zc * zc, axis=-1, keepdims=True) + EPS) * lng_ref[...]
               + lnb_ref[...]).astype(BF16)
        hd = zvn.shape[1] // SGU_HEADS
        chunks = []
        for c in range(TOKEN_SUB // CHUNK):
            blk = zvn[c * CHUNK:(c + 1) * CHUNK]
            chunks.append(jnp.concatenate(
                [_dot(ws_ref[h], blk[:, h * hd:(h + 1) * hd]) for h in range(SGU_HEADS)], axis=1)
                + bias_ref[...])
        sv = jnp.concatenate(chunks, axis=0)
        zu = jax.nn.gelu(_dot(n, wu_ref[...]))
        yb = (zu * sv).astype(BF16)
        gb = jax.nn.sigmoid(_dot(n, wgb_ref[...]))
        pb_ref[rows, :] = (gb * _dot(yb, wpb_ref[...])).astype(BF16)

    _sub_loop(x_ref.shape[0], body)


def _in_proj(x2, g, wat, wu, wv, wga, wgb, lng, lnb, ws, bias_full, wpb):
    t, d = x2.shape
    sw = wat.shape[0]
    tm = TOKEN_BLOCK
    consts = (g, wat, wu, wv, wga, wgb, lng, lnb, ws, bias_full, wpb)
    return pl.pallas_call(
        _in_proj_kernel,
        grid=(t // tm,),
        in_specs=[pl.BlockSpec((tm, d), lambda i: (i, 0))] + [_const_spec(c.shape) for c in consts],
        out_specs=[pl.BlockSpec((sw, tm // CHUNK, CHUNK), lambda i: (0, i, 0)),
                   pl.BlockSpec((tm, d), lambda i: (i, 0)),
                   pl.BlockSpec((tm, d), lambda i: (i, 0))],
        out_shape=[jax.ShapeDtypeStruct((sw, t // CHUNK, CHUNK), F32),
                   jax.ShapeDtypeStruct((t, d), BF16),
                   jax.ShapeDtypeStruct((t, d), BF16)],
        compiler_params=_cparams("arbitrary"),
        name="in_proj",
    )(x2, *consts)


def _cmul(ar, ai, br, bi):
    return ar * br - ai * bi, ar * bi + ai * br


def _cpow(br, bi, e, nbits):
    rr = ri = None
    for k in range(nbits):
        bit = ((e >> k) & 1) == 1
        sr, si = jnp.where(bit, br, 1.0), jnp.where(bit, bi, 0.0)
        rr, ri = (sr, si) if rr is None else _cmul(rr, ri, sr, si)
        if k + 1 < nbits:
            br, bi = _cmul(br, bi, br, bi)
    return rr, ri


class _S5Tabs(NamedTuple):
    kf: tuple
    kb: tuple
    lt: tuple
    et: tuple
    bb: tuple
    cc: tuple
    a: tuple


def _s5_tables_in_kernel(row_ref, col_ref, bc_ref):
    gc, half, ns = S5_GROUP, S5_GROUP // 2, 2 * S5_STATE
    sq = (CHUNK, CHUNK)
    sub = lax.broadcasted_iota(jnp.int32, sq, 0)
    lan = lax.broadcasted_iota(jnp.int32, sq, 1)
    lb_re, lb_im = row_ref[0, 0:1, :], row_ref[0, 1:2, :]
    q_re, q_im = row_ref[0, 2:3, :], row_ref[0, 3:4, :]
    fwd_lane = lan < S5_STATE
    lt = _cpow(lb_re, lb_im, jnp.where(fwd_lane, CHUNK - 1 - sub, sub), 7)
    et = _cpow(lb_re, lb_im, jnp.where(fwd_lane, sub + 1, CHUNK - sub), 8)
    p_re, p_im = _cpow(col_ref[0, :, 0:1], col_ref[0, :, 1:2],
                       jnp.where(sub < S5_STATE, lan, (CHUNK - lan) & (CHUNK - 1)), 7)
    a_re, a_im = lb_re, lb_im
    for _ in range(7):
        a_re, a_im = _cmul(a_re, a_im, a_re, a_im)
    bb = _cmul(q_re, q_im, bc_ref[0, 0], bc_ref[0, 1])
    c_re, c_im = bc_ref[0, 2], bc_ref[0, 3]

    rhs_f = jnp.concatenate([p_re[:S5_STATE], p_im[:S5_STATE]], axis=0)
    rhs_b = jnp.concatenate([p_re[S5_STATE:], p_im[S5_STATE:]], axis=0)
    lane_f = lax.broadcasted_iota(jnp.int32, (gc * half, ns), 1) < S5_STATE
    lane_tap = lax.broadcasted_iota(jnp.int32, (gc * half, CHUNK), 1)
    taps = []
    for par in range(2):
        cpr, cpi = c_re[par * half:(par + 1) * half], c_im[par * half:(par + 1) * half]
        cb = [_cmul(cpr, cpi, bb[0][ci:ci + 1], bb[1][ci:ci + 1]) for ci in range(gc)]
        cb_re = jnp.concatenate([v[0] for v in cb], axis=0)
        cb_im = jnp.concatenate([v[1] for v in cb], axis=0)
        lhs_f = jnp.where(lane_f, cb_re, -pltpu.roll(cb_im, S5_STATE, 1))
        lhs_b = jnp.where(lane_f, pltpu.roll(cb_re, S5_STATE, 1), -cb_im)
        kf = jnp.dot(lhs_f, rhs_f, preferred_element_type=F32, precision=lax.Precision.HIGHEST)
        kb = jnp.dot(lhs_b, rhs_b, preferred_element_type=F32, precision=lax.Precision.HIGHEST)
        kf = kf + jnp.where(lane_tap == 0, kb[:, 0:1] + col_ref[0, :, 2 + par:3 + par], 0.0)
        taps.append((kf, kb))

    return _S5Tabs((taps[0][0], taps[1][0]), (taps[0][1], taps[1][1]),
                   lt, et, bb, (c_re, c_im), (a_re, a_im))


def _s5_build_small(tabs, slot, bm_ref, cm_ref, a_ref):
    half = S5_GROUP // 2
    for ch in range(S5_GROUP):
        w_re, w_im = _cmul(*tabs.lt, tabs.bb[0][ch:ch + 1], tabs.bb[1][ch:ch + 1])
        bm_ref[slot, ch * CHUNK:(ch + 1) * CHUNK, :] = jnp.concatenate([w_re, w_im], axis=1).astype(BF16)
        r = (ch % 2) * half + ch // 2
        g_re, g_im = _cmul(*tabs.et, tabs.cc[0][r:r + 1], tabs.cc[1][r:r + 1])
        cm_ref[slot, ch * CHUNK:(ch + 1) * CHUNK, :] = jnp.concatenate([g_re, -g_im], axis=1).astype(BF16)
    a_ref[slot, 0] = jnp.broadcast_to(tabs.a[0], a_ref.shape[2:])
    a_ref[slot, 1] = jnp.broadcast_to(tabs.a[1], a_ref.shape[2:])


def _s5_build_pair(tabs, slot, q, m_ref):
    half = S5_GROUP // 2
    sq = (CHUNK, CHUNK)
    upper = lax.broadcasted_iota(jnp.int32, sq, 1) >= lax.broadcasted_iota(jnp.int32, sq, 0)
    for ci in range(S5_GROUP):
        r = ci * half + q
        blocks = []
        for par in range(2):
            vf = jnp.broadcast_to(tabs.kf[par][r:r + 1], sq)
            vb = jnp.broadcast_to(tabs.kb[par][r:r + 1], sq)
            cf = pltpu.roll(vf, 0, 1, stride=1, stride_axis=0)
            cbw = pltpu.roll(vb, 0, 1, stride=1, stride_axis=0)
            blocks.append(jnp.where(upper, cf, cbw))
        m_ref[slot, ci * CHUNK:(ci + 1) * CHUNK, 2 * q * CHUNK:(2 * q + 2) * CHUNK] = (
            jnp.concatenate(blocks, axis=1).astype(BF16))


def _s5_apply(x_ref, y_ref, gi, slot, m_ref, bm_ref, cm_ref, a_ref, s_ref, xf_ref, xb_ref, nbatch,
              between):
    gc, ns = S5_GROUP, 2 * S5_STATE
    nrows = x_ref.shape[1]
    nch = nrows // nbatch
    x = jnp.concatenate([x_ref[gi * gc + ci].astype(BF16) for ci in range(gc)], axis=1)
    s = _dot(x, bm_ref[slot])
    s_ref[0] = s[:, :ns]
    s_ref[1] = s[:, ns:]
    a_re, a_im = a_ref[slot, 0, 0:1, :], a_ref[slot, 1, 0:1, :]
    is_fwd = lax.broadcasted_iota(jnp.int32, (nbatch, ns), 1) < S5_STATE
    st_re = st_im = jnp.zeros((nbatch, ns), F32)
    for i in range(nch):
        rows_f = pl.ds(i, nbatch, stride=nch)
        rows_b = pl.ds(nch - 1 - i, nbatch, stride=nch)
        xf_ref[0, rows_f, :] = st_re
        xf_ref[1, rows_f, :] = st_im
        xb_ref[0, rows_b, :] = st_re
        xb_ref[1, rows_b, :] = st_im
        if i + 1 < nch:
            s_re = jnp.where(is_fwd, s_ref[0, rows_f, :], s_ref[0, rows_b, :])
            s_im = jnp.where(is_fwd, s_ref[1, rows_f, :], s_ref[1, rows_b, :])
            n_re, n_im = _cmul(a_re, a_im, st_re, st_im)
            st_re, st_im = n_re + s_re, n_im + s_im
    is_fwd_all = lax.broadcasted_iota(jnp.int32, (nrows, ns), 1) < S5_STATE
    xin = jnp.concatenate([jnp.where(is_fwd_all, xf_ref[0], xb_ref[0]),
                           jnp.where(is_fwd_all, xf_ref[1], xb_ref[1])], axis=1).astype(BF16)
    for q in range(gc // 2):
        between(q)
        cols = slice(2 * q * CHUNK, (2 * q + 2) * CHUNK)
        y = _dot(x, m_ref[slot, :, cols]) + _dot_nt(xin, cm_ref[slot, cols, :])
        y_ref[gi * gc + 2 * q] = y[:, :CHUNK]
        y_ref[gi * gc + 2 * q + 1] = y[:, CHUNK:]


def _s5_kernel(x_ref, row0, col0, bc0, row1, col1, bc1,
               y_ref, m_ref, bm_ref, cm_ref, a_ref, s_ref, xf_ref, xb_ref, *, nbatch):
    scr = (m_ref, bm_ref, cm_ref, a_ref, s_ref, xf_ref, xb_ref)
    k = pl.program_id(0)

    @pl.when(k == 0)
    def _():
        tabs = _s5_tables_in_kernel(row0, col0, bc0)
        _s5_build_small(tabs, 0, bm_ref, cm_ref, a_ref)
        for q in range(S5_GROUP // 2):
            _s5_build_pair(tabs, 0, q, m_ref)

    for cur in range(2):
        @pl.when(k % 2 == cur)
        def _():
            tabs = _s5_tables_in_kernel(row1, col1, bc1)
            _s5_build_small(tabs, 1 - cur, bm_ref, cm_ref, a_ref)
            _s5_apply(x_ref, y_ref, 0, cur, *scr, nbatch,
                      functools.partial(_s5_build_pair, tabs, 1 - cur, m_ref=m_ref))


def _s5(xat3, tabs, nbatch):
    sw, nrows, _ = xat3.shape
    gc, ns = S5_GROUP, 2 * S5_STATE
    nsteps = sw // gc
    last = nsteps - 1

    def tspecs(index):
        return [pl.BlockSpec((1,) + a.shape[1:], lambda k, n=a.ndim: (index(k),) + (0,) * (n - 1))
                for a in tabs]

    return pl.pallas_call(
        functools.partial(_s5_kernel, nbatch=nbatch),
        grid=(nsteps,),
        in_specs=[pl.BlockSpec((gc, nrows, CHUNK), lambda k: (k, 0, 0))]
                 + tspecs(lambda k: 0) + tspecs(lambda k: jnp.minimum(k + 1, last)),
        out_specs=pl.BlockSpec((gc, nrows, CHUNK), lambda k: (k, 0, 0)),
        out_shape=jax.ShapeDtypeStruct((sw, nrows, CHUNK), F32),
        scratch_shapes=[pltpu.VMEM((2, gc * CHUNK, gc * CHUNK), BF16),
                        pltpu.VMEM((2, gc * CHUNK, 2 * ns), BF16),
                        pltpu.VMEM((2, gc * CHUNK, 2 * ns), BF16),
                        pltpu.VMEM((2, 2, 8, ns), F32),
                        pltpu.VMEM((2, nrows, ns), F32),
                        pltpu.VMEM((2, nrows, ns), F32),
                        pltpu.VMEM((2, nrows, ns), F32)],
        compiler_params=_cparams("arbitrary"),
        name="s5",
    )(xat3, *tabs, *tabs)


def _s5_tables(lam_re, lam_im, log_step, b_re, b_im, c_re, c_im, d):
    ng, npar, nc = b_re.shape[1], b_re.shape[2], b_re.shape[3]
    half = nc // 2
    lam_re, lam_im = lam_re.astype(F32), lam_im.astype(F32)
    step = jnp.exp(log_step.astype(F32))[..., None]
    mag = jnp.exp(lam_re * step)
    lb_re, lb_im = mag * jnp.cos(lam_im * step), mag * jnp.sin(lam_im * step)
    den = lam_re * lam_re + lam_im * lam_im
    q_re = ((lb_re - 1.0) * lam_re + lb_im * lam_im) / den
    q_im = (lb_im * lam_re - (lb_re - 1.0) * lam_im) / den
    lanes = lambda a: a.transpose(1, 0, 2).reshape(ng, 2 * npar)
    rows = jnp.stack([lanes(a) for a in (lb_re, lb_im, q_re, q_im)], axis=1)
    rows = jnp.pad(rows, ((0, 0), (0, 4), (0, 0)))
    ci = jnp.arange(nc)[:, None, None]
    co = 2 * jnp.arange(half)[None, :, None] + jnp.arange(2)[None, None, :]
    dsel = (ci == co).astype(F32).reshape(nc * half, 2)
    dcol = jnp.repeat(d.astype(F32).reshape(ng, nc), half, axis=1)[..., None] * dsel[None]
    cols = jnp.concatenate([rows[:, :2].transpose(0, 2, 1), dcol], axis=2)
    perm = jnp.concatenate([jnp.arange(0, nc, 2), jnp.arange(1, nc, 2)])
    bc = jnp.stack([b_re.astype(F32).transpose(1, 3, 0, 2).reshape(ng, nc, 2 * npar),
                    b_im.astype(F32).transpose(1, 3, 0, 2).reshape(ng, nc, 2 * npar),
                    c_re.astype(F32).transpose(1, 2, 0, 3).reshape(ng, nc, 2 * npar)[:, perm],
                    c_im.astype(F32).transpose(1, 2, 0, 3).reshape(ng, nc, 2 * npar)[:, perm]],
                   axis=1)
    return rows, cols, bc


def _merge_kernel(yt_ref, ga_ref, pb_ref, x_ref, wglut_ref, wpa_ref, wout_ref, h_ref):
    def body(r0, sub):
        rows = pl.ds(r0, TOKEN_SUB)
        ncs = TOKEN_SUB // CHUNK
        y = jax.nn.gelu(jnp.concatenate([yt_ref[:, sub * ncs + c, :] for c in range(ncs)], axis=1))
        gl = _dot(wglut_ref[...], y.astype(BF16))
        ya = (y * jax.nn.sigmoid(gl)).T.astype(BF16)
        merged = ga_ref[rows, :].astype(F32) * _dot(ya, wpa_ref[...]) + pb_ref[rows, :].astype(F32)
        h_ref[rows, :] = x_ref[rows, :] + _dot(merged.astype(BF16), wout_ref[...])

    _sub_loop(x_ref.shape[0], body)


def _merge(yt3, ga, pb, x2, wglut, wpa, wout):
    t, d = x2.shape
    sw = yt3.shape[0]
    tm = TOKEN_BLOCK
    return pl.pallas_call(
        _merge_kernel,
        grid=(t // tm,),
        in_specs=[pl.BlockSpec((sw, tm // CHUNK, CHUNK), lambda i: (0, i, 0)),
                  pl.BlockSpec((tm, d), lambda i: (i, 0)),
                  pl.BlockSpec((tm, d), lambda i: (i, 0)),
                  pl.BlockSpec((tm, d), lambda i: (i, 0)),
                  _const_spec(wglut.shape), _const_spec(wpa.shape), _const_spec(wout.shape)],
        out_specs=pl.BlockSpec((tm, d), lambda i: (i, 0)),
        out_shape=jax.ShapeDtypeStruct((t, d), F32),
        compiler_params=_cparams("arbitrary"),
        name="merge",
    )(yt3, ga, pb, x2, wglut, wpa, wout)


def _xattn_kernel(h_ref, g_ref, wq_ref, k_ref, v_ref, wo_ref, o_ref):
    h = h_ref[0]
    hn = _rms(h, g_ref[...]).astype(BF16)
    q = _dot(hn, wq_ref[...]).astype(BF16)
    hd = q.shape[1] // XATTN_HEADS
    outs = []
    for hh in range(XATTN_HEADS):
        sl = slice(hh * hd, (hh + 1) * hd)
        s = _dot_nt(q[:, sl], k_ref[0, :, sl])
        e = jnp.exp(s - jnp.max(s, axis=-1, keepdims=True))
        p = e / jnp.sum(e, axis=-1, keepdims=True)
        outs.append(_dot(p.astype(BF16), v_ref[0, :, sl]))
    o = jnp.concatenate(outs, axis=1).astype(BF16)
    o_ref[0] = h + _dot(o, wo_ref[...])


def _xattn(h3, g, wq, k, v, wo, tm):
    b, l, d = h3.shape
    m = k.shape[1]
    return pl.pallas_call(
        _xattn_kernel,
        grid=(b, l // tm),
        in_specs=[pl.BlockSpec((1, tm, d), lambda i, j: (i, j, 0)),
                  _const_spec(g.shape), _const_spec(wq.shape),
                  pl.BlockSpec((1, m, d), lambda i, j: (i, 0, 0)),
                  pl.BlockSpec((1, m, d), lambda i, j: (i, 0, 0)),
                  _const_spec(wo.shape)],
        out_specs=pl.BlockSpec((1, tm, d), lambda i, j: (i, j, 0)),
        out_shape=jax.ShapeDtypeStruct((b, l, d), F32),
        compiler_params=_cparams("arbitrary", "arbitrary"),
        name="xattn",
    )(h3, g, wq, k, v, wo)


def _ffn_kernel(h_ref, g_ref, wg_ref, wu_ref, wd_ref, gf_ref, o_ref):
    h = h_ref[...]
    hn = _rms(h, g_ref[...]).astype(BF16)
    act = (jax.nn.silu(_dot(hn, wg_ref[...])) * _dot(hn, wu_ref[...])).astype(BF16)
    o_ref[...] = _rms(h + _dot(act, wd_ref[...]), gf_ref[...])


def _ffn(h2, g, wg, wu, wd, gf, tm):
    t, d = h2.shape
    consts = (g, wg, wu, wd, gf)
    return pl.pallas_call(
        _ffn_kernel,
        grid=(t // tm,),
        in_specs=[pl.BlockSpec((tm, d), lambda i: (i, 0))] + [_const_spec(c.shape) for c in consts],
        out_specs=pl.BlockSpec((tm, d), lambda i: (i, 0)),
        out_shape=jax.ShapeDtypeStruct((t, d), F32),
        compiler_params=_cparams("arbitrary"),
        name="ffn",
    )(h2, *consts)


def _layer(h, kmem, vmem, mix_g, w_in, s5p, w_glu, ln_g, ln_b, sgu_w, sgu_bias, wpa, wpb, wout,
           xg, wq, wxo, fg, wgate, wup, wdown, final_g):
    b, l, d = h.shape
    t = b * l
    sw = w_glu.shape[0]
    su = ln_g.shape[0]
    off_u, off_v, off_ga, off_gb = sw, sw + su, sw + 2 * su, sw + 2 * su + d
    row = lambda a: a.astype(F32).reshape(1, -1)
    x2 = h.reshape(t, d)
    bias_full = jnp.repeat(sgu_bias.astype(F32).T, su // SGU_HEADS, axis=1)
    xat3, ga, pb = _in_proj(
        x2, row(mix_g), w_in[:, :off_u].T.astype(BF16), w_in[:, off_u:off_v].astype(BF16),
        w_in[:, off_v:off_ga].astype(BF16), w_in[:, off_ga:off_gb].astype(BF16),
        w_in[:, off_gb:].astype(BF16), row(ln_g), row(ln_b), sgu_w.astype(BF16),
        bias_full, wpb.astype(BF16))
    yt3 = _s5(xat3, _s5_tables(*s5p), b)
    h1 = _merge(yt3, ga, pb, x2, w_glu.T.astype(BF16), wpa.astype(BF16), wout.astype(BF16))
    h2 = _xattn(h1.reshape(b, l, d), row(xg), wq.astype(BF16), kmem, vmem, wxo.astype(BF16), 256)
    out = _ffn(h2.reshape(t, d), row(fg), wgate.astype(BF16), wup.astype(BF16), wdown.astype(BF16),
               row(final_g), 256)
    return out.reshape(b, l, d)


def kernel(x, mem, mix_norm_g, w_in, s5_lam_re, s5_lam_im, s5_log_step, s5_b_re, s5_b_im, s5_c_re, s5_c_im, s5_d, s5_w_glu, sgu_ln_g, sgu_ln_b, sgu_w, sgu_bias, w_proj_a, w_proj_b, w_out, xattn_norm_g, mem_norm_g, w_q, w_k, w_v, w_xo, ffn_norm_g, w_gate, w_up, w_down, final_norm_g):
    depth = w_in.shape[0]
    assert depth == 1, "the final rms_norm is fused into the last layer's ffn kernel"
    hd = x.shape[-1] // XATTN_HEADS
    i = 0
    kmem, vmem = _mem_kv(mem, mem_norm_g.astype(F32).reshape(1, -1), w_k[i].astype(BF16),
                         w_v[i].astype(BF16), hd ** -0.5)
    s5p = (s5_lam_re[i], s5_lam_im[i], s5_log_step[i], s5_b_re[i], s5_b_im[i], s5_c_re[i],
           s5_c_im[i], s5_d[i])
    return _layer(x, kmem, vmem, mix_norm_g[i], w_in[i], s5p, s5_w_glu[i], sgu_ln_g[i], sgu_ln_b[i],
                  sgu_w[i], sgu_bias[i], w_proj_a[i], w_proj_b[i], w_out[i], xattn_norm_g[i], w_q[i],
                  w_xo[i], ffn_norm_g[i], w_gate[i], w_up[i], w_down[i], final_norm_g)
```

```python
import functools
from typing import NamedTuple

import jax
import jax.numpy as jnp
from jax import lax
from jax.experimental import pallas as pl
from jax.experimental.pallas import tpu as pltpu

F32 = jnp.float32
BF16 = jnp.bfloat16

EPS = 1e-6
LANES = 128
CHUNK = 128
S5_GROUP = 16
S5_STATE = 64
SGU_HEADS = 8
XATTN_HEADS = 4
VMEM_LIMIT = 56 * 1024 * 1024
TOKEN_BLOCK = 1024
TOKEN_SUB = 256


def _cparams(*sem):
    return pltpu.CompilerParams(dimension_semantics=sem, vmem_limit_bytes=VMEM_LIMIT)


def _const_spec(shape):
    nd = len(shape)
    return pl.BlockSpec(shape, lambda *_: (0,) * nd, pipeline_mode=pl.Buffered(1))


def _rms(x, g):
    return x * lax.rsqrt(jnp.mean(x * x, axis=-1, keepdims=True) + EPS) * g


def _dot(a, b):
    return jnp.dot(a, b, preferred_element_type=F32)


def _dot_nt(a, b):
    return lax.dot_general(a, b, (((1,), (1,)), ((), ())), preferred_element_type=F32)


def _sub_loop(n_tokens, body):
    def step(sub, carry):
        body(pl.multiple_of(sub * TOKEN_SUB, TOKEN_SUB), sub)
        return carry
    lax.fori_loop(0, n_tokens // TOKEN_SUB, step, 0, unroll=2)


def _mem_kv_kernel(mem_ref, g_ref, wk_ref, wv_ref, k_ref, v_ref, *, scale):
    mn = _rms(mem_ref[0], g_ref[...]).astype(BF16)
    k_ref[0] = (_dot(mn, wk_ref[...]) * scale).astype(BF16)
    v_ref[0] = _dot(mn, wv_ref[...]).astype(BF16)


def _mem_kv(mem, g, wk, wv, scale):
    b, m, d = mem.shape
    return pl.pallas_call(
        functools.partial(_mem_kv_kernel, scale=scale),
        grid=(b,),
        in_specs=[pl.BlockSpec((1, m, d), lambda i: (i, 0, 0)),
                  _const_spec((1, d)), _const_spec((d, d)), _const_spec((d, d))],
        out_specs=[pl.BlockSpec((1, m, d), lambda i: (i, 0, 0))] * 2,
        out_shape=[jax.ShapeDtypeStruct((b, m, d), BF16)] * 2,
        compiler_params=_cparams("arbitrary"),
        name="mem_kv",
    )(mem, g, wk, wv)


def _in_proj_kernel(x_ref, g_ref, wat_ref, wu_ref, wv_ref, wga_ref, wgb_ref, lng_ref, lnb_ref,
                    ws_ref, bias_ref, wpb_ref, xat_ref, ga_ref, pb_ref):
    def body(r0, sub):
        rows = pl.ds(r0, TOKEN_SUB)
        n = _rms(x_ref[rows, :], g_ref[...]).astype(BF16)
        xat = _dot_nt(wat_ref[...], n)
        for c in range(TOKEN_SUB // CHUNK):
            xat_ref[:, sub * (TOKEN_SUB // CHUNK) + c, :] = xat[:, c * CHUNK:(c + 1) * CHUNK]
        ga_ref[rows, :] = jax.nn.sigmoid(_dot(n, wga_ref[...])).astype(BF16)
        zv = jax.nn.gelu(_dot(n, wv_ref[...]))
        zc = zv - jnp.mean(zv, axis=-1, keepdims=True)
        zvn = (zc * lax.rsqrt(jnp.mean(zc * zc, axis=-1, keepdims=True) + EPS) * lng_ref[...]
               + lnb_ref[...]).astype(BF16)
        hd = zvn.shape[1] // SGU_HEADS
        ncs = TOKEN_SUB // CHUNK
        mixed = [_dot(ws_ref[h], jnp.concatenate(
            [zvn[c * CHUNK:(c + 1) * CHUNK, h * hd:(h + 1) * hd] for c in range(ncs)], axis=1))
            for h in range(SGU_HEADS)]
        sv = jnp.concatenate(
            [jnp.concatenate([m[:, c * hd:(c + 1) * hd] for m in mixed], axis=1) + bias_ref[...]
             for c in range(ncs)], axis=0)
        zu = jax.nn.gelu(_dot(n, wu_ref[...]))
        yb = (zu * sv).astype(BF16)
        gb = jax.nn.sigmoid(_dot(n, wgb_ref[...]))
        pb_ref[rows, :] = (gb * _dot(yb, wpb_ref[...])).astype(BF16)

    _sub_loop(x_ref.shape[0], body)


def _in_proj(x2, g, wat, wu, wv, wga, wgb, lng, lnb, ws, bias_full, wpb):
    t, d = x2.shape
    sw = wat.shape[0]
    tm = TOKEN_BLOCK
    consts = (g, wat, wu, wv, wga, wgb, lng, lnb, ws, bias_full, wpb)
    return pl.pallas_call(
        _in_proj_kernel,
        grid=(t // tm,),
        in_specs=[pl.BlockSpec((tm, d), lambda i: (i, 0))] + [_const_spec(c.shape) for c in consts],
        out_specs=[pl.BlockSpec((sw, tm // CHUNK, CHUNK), lambda i: (0, i, 0)),
                   pl.BlockSpec((tm, d), lambda i: (i, 0)),
                   pl.BlockSpec((tm, d), lambda i: (i, 0))],
        out_shape=[jax.ShapeDtypeStruct((sw, t // CHUNK, CHUNK), F32),
                   jax.ShapeDtypeStruct((t, d), BF16),
                   jax.ShapeDtypeStruct((t, d), BF16)],
        compiler_params=_cparams("arbitrary"),
        name="in_proj",
    )(x2, *consts)


def _cmul(ar, ai, br, bi):
    return ar * br - ai * bi, ar * bi + ai * br


def _cpow(br, bi, e, nbits):
    rr = ri = None
    for k in range(nbits):
        bit = ((e >> k) & 1) == 1
        sr, si = jnp.where(bit, br, 1.0), jnp.where(bit, bi, 0.0)
        rr, ri = (sr, si) if rr is None else _cmul(rr, ri, sr, si)
        if k + 1 < nbits:
            br, bi = _cmul(br, bi, br, bi)
    return rr, ri


class _S5Tabs(NamedTuple):
    kf: tuple
    kb: tuple
    lt: tuple
    et: tuple
    bb: tuple
    cc: tuple
    a: tuple


def _s5_tables_in_kernel(row_ref, col_ref, bc_ref):
    gc, half, ns = S5_GROUP, S5_GROUP // 2, 2 * S5_STATE
    sq = (CHUNK, CHUNK)
    sub = lax.broadcasted_iota(jnp.int32, sq, 0)
    lan = lax.broadcasted_iota(jnp.int32, sq, 1)
    lb_re, lb_im = row_ref[0, 0:1, :], row_ref[0, 1:2, :]
    q_re, q_im = row_ref[0, 2:3, :], row_ref[0, 3:4, :]
    fwd_lane = lan < S5_STATE
    lt = _cpow(lb_re, lb_im, jnp.where(fwd_lane, CHUNK - 1 - sub, sub), 7)
    et = _cpow(lb_re, lb_im, jnp.where(fwd_lane, sub + 1, CHUNK - sub), 8)
    p_re, p_im = _cpow(col_ref[0, :, 0:1], col_ref[0, :, 1:2],
                       jnp.where(sub < S5_STATE, lan, (CHUNK - lan) & (CHUNK - 1)), 7)
    a_re, a_im = lb_re, lb_im
    for _ in range(7):
        a_re, a_im = _cmul(a_re, a_im, a_re, a_im)
    bb = _cmul(q_re, q_im, bc_ref[0, 0], bc_ref[0, 1])
    c_re, c_im = bc_ref[0, 2], bc_ref[0, 3]

    rhs_f = jnp.concatenate([p_re[:S5_STATE], p_im[:S5_STATE]], axis=0)
    rhs_b = jnp.concatenate([p_re[S5_STATE:], p_im[S5_STATE:]], axis=0)
    lane_f = lax.broadcasted_iota(jnp.int32, (gc * half, ns), 1) < S5_STATE
    lane_tap = lax.broadcasted_iota(jnp.int32, (gc * half, CHUNK), 1)
    taps = []
    for par in range(2):
        cpr, cpi = c_re[par * half:(par + 1) * half], c_im[par * half:(par + 1) * half]
        cb = [_cmul(cpr, cpi, bb[0][ci:ci + 1], bb[1][ci:ci + 1]) for ci in range(gc)]
        cb_re = jnp.concatenate([v[0] for v in cb], axis=0)
        cb_im = jnp.concatenate([v[1] for v in cb], axis=0)
        lhs_f = jnp.where(lane_f, cb_re, -pltpu.roll(cb_im, S5_STATE, 1))
        lhs_b = jnp.where(lane_f, pltpu.roll(cb_re, S5_STATE, 1), -cb_im)
        kf = jnp.dot(lhs_f, rhs_f, preferred_element_type=F32, precision=lax.Precision.HIGHEST)
        kb = jnp.dot(lhs_b, rhs_b, preferred_element_type=F32, precision=lax.Precision.HIGHEST)
        kf = kf + jnp.where(lane_tap == 0, kb[:, 0:1] + col_ref[0, :, 2 + par:3 + par], 0.0)
        taps.append((kf, kb))

    return _S5Tabs((taps[0][0], taps[1][0]), (taps[0][1], taps[1][1]),
                   lt, et, bb, (c_re, c_im), (a_re, a_im))


def _s5_build_small(tabs, slot, bm_ref, cm_ref, a_ref):
    half = S5_GROUP // 2
    for ch in range(S5_GROUP):
        w_re, w_im = _cmul(*tabs.lt, tabs.bb[0][ch:ch + 1], tabs.bb[1][ch:ch + 1])
        bm_ref[slot, ch * CHUNK:(ch + 1) * CHUNK, :] = jnp.concatenate([w_re, w_im], axis=1).astype(BF16)
        r = (ch % 2) * half + ch // 2
        g_re, g_im = _cmul(*tabs.et, tabs.cc[0][r:r + 1], tabs.cc[1][r:r + 1])
        cm_ref[slot, ch * CHUNK:(ch + 1) * CHUNK, :] = jnp.concatenate([g_re, -g_im], axis=1).astype(BF16)
    a_ref[slot, 0] = jnp.broadcast_to(tabs.a[0], a_ref.shape[2:])
    a_ref[slot, 1] = jnp.broadcast_to(tabs.a[1], a_ref.shape[2:])


def _s5_build_pair(tabs, slot, q, m_ref):
    half = S5_GROUP // 2
    sq = (CHUNK, CHUNK)
    upper = lax.broadcasted_iota(jnp.int32, sq, 1) >= lax.broadcasted_iota(jnp.int32, sq, 0)
    for ci in range(S5_GROUP):
        r = ci * half + q
        blocks = []
        for par in range(2):
            vf = jnp.broadcast_to(tabs.kf[par][r:r + 1], sq)
            vb = jnp.broadcast_to(tabs.kb[par][r:r + 1], sq)
            cf = pltpu.roll(vf, 0, 1, stride=1, stride_axis=0)
            cbw = pltpu.roll(vb, 0, 1, stride=1, stride_axis=0)
            blocks.append(jnp.where(upper, cf, cbw))
        m_ref[slot, ci * CHUNK:(ci + 1) * CHUNK, 2 * q * CHUNK:(2 * q + 2) * CHUNK] = (
            jnp.concatenate(blocks, axis=1).astype(BF16))


def _s5_apply(x_ref, y_ref, gi, slot, m_ref, bm_ref, cm_ref, a_ref, s_ref, xf_ref, xb_ref, nbatch,
              between):
    gc, ns = S5_GROUP, 2 * S5_STATE
    nrows = x_ref.shape[1]
    nch = nrows // nbatch
    x = jnp.concatenate([x_ref[gi * gc + ci].astype(BF16) for ci in range(gc)], axis=1)
    s = _dot(x, bm_ref[slot])
    s_ref[0] = s[:, :ns]
    s_ref[1] = s[:, ns:]
    a_re, a_im = a_ref[slot, 0, 0:1, :], a_ref[slot, 1, 0:1, :]
    is_fwd = lax.broadcasted_iota(jnp.int32, (nbatch, ns), 1) < S5_STATE
    st_re = st_im = jnp.zeros((nbatch, ns), F32)
    for i in range(nch):
        rows_f = pl.ds(i, nbatch, stride=nch)
        rows_b = pl.ds(nch - 1 - i, nbatch, stride=nch)
        xf_ref[0, rows_f, :] = st_re
        xf_ref[1, rows_f, :] = st_im
        xb_ref[0, rows_b, :] = st_re
        xb_ref[1, rows_b, :] = st_im
        if i + 1 < nch:
            s_re = jnp.where(is_fwd, s_ref[0, rows_f, :], s_ref[0, rows_b, :])
            s_im = jnp.where(is_fwd, s_ref[1, rows_f, :], s_ref[1, rows_b, :])
            n_re, n_im = _cmul(a_re, a_im, st_re, st_im)
            st_re, st_im = n_re + s_re, n_im + s_im
    is_fwd_all = lax.broadcasted_iota(jnp.int32, (nrows, ns), 1) < S5_STATE
    xin = jnp.concatenate([jnp.where(is_fwd_all, xf_ref[0], xb_ref[0]),
                           jnp.where(is_fwd_all, xf_ref[1], xb_ref[1])], axis=1).astype(BF16)
    for q in range(gc // 2):
        between(q)
        cols = slice(2 * q * CHUNK, (2 * q + 2) * CHUNK)
        y = _dot(x, m_ref[slot, :, cols]) + _dot_nt(xin, cm_ref[slot, cols, :])
        y_ref[gi * gc + 2 * q] = y[:, :CHUNK]
        y_ref[gi * gc + 2 * q + 1] = y[:, CHUNK:]


def _s5_kernel(x_ref, row0, col0, bc0, row1, col1, bc1,
               y_ref, m_ref, bm_ref, cm_ref, a_ref, s_ref, xf_ref, xb_ref, *, nbatch):
    scr = (m_ref, bm_ref, cm_ref, a_ref, s_ref, xf_ref, xb_ref)
    k = pl.program_id(0)

    @pl.when(k == 0)
    def _():
        tabs = _s5_tables_in_kernel(row0, col0, bc0)
        _s5_build_small(tabs, 0, bm_ref, cm_ref, a_ref)
        for q in range(S5_GROUP // 2):
            _s5_build_pair(tabs, 0, q, m_ref)

    for cur in range(2):
        @pl.when(k % 2 == cur)
        def _():
            tabs = _s5_tables_in_kernel(row1, col1, bc1)
            _s5_build_small(tabs, 1 - cur, bm_ref, cm_ref, a_ref)
            _s5_apply(x_ref, y_ref, 0, cur, *scr, nbatch,
                      functools.partial(_s5_build_pair, tabs, 1 - cur, m_ref=m_ref))


def _s5(xat3, tabs, nbatch):
    sw, nrows, _ = xat3.shape
    gc, ns = S5_GROUP, 2 * S5_STATE
    nsteps = sw // gc
    last = nsteps - 1

    def tspecs(index):
        return [pl.BlockSpec((1,) + a.shape[1:], lambda k, n=a.ndim: (index(k),) + (0,) * (n - 1))
                for a in tabs]

    return pl.pallas_call(
        functools.partial(_s5_kernel, nbatch=nbatch),
        grid=(nsteps,),
        in_specs=[pl.BlockSpec((gc, nrows, CHUNK), lambda k: (k, 0, 0))]
                 + tspecs(lambda k: 0) + tspecs(lambda k: jnp.minimum(k + 1, last)),
        out_specs=pl.BlockSpec((gc, nrows, CHUNK), lambda k: (k, 0, 0)),
        out_shape=jax.ShapeDtypeStruct((sw, nrows, CHUNK), F32),
        scratch_shapes=[pltpu.VMEM((2, gc * CHUNK, gc * CHUNK), BF16),
                        pltpu.VMEM((2, gc * CHUNK, 2 * ns), BF16),
                        pltpu.VMEM((2, gc * CHUNK, 2 * ns), BF16),
                        pltpu.VMEM((2, 2, 8, ns), F32),
                        pltpu.VMEM((2, nrows, ns), F32),
                        pltpu.VMEM((2, nrows, ns), F32),
                        pltpu.VMEM((2, nrows, ns), F32)],
        compiler_params=_cparams("arbitrary"),
        name="s5",
    )(xat3, *tabs, *tabs)


def _s5_tables(lam_re, lam_im, log_step, b_re, b_im, c_re, c_im, d):
    ng, npar, nc = b_re.shape[1], b_re.shape[2], b_re.shape[3]
    half = nc // 2
    lam_re, lam_im = lam_re.astype(F32), lam_im.astype(F32)
    step = jnp.exp(log_step.astype(F32))[..., None]
    mag = jnp.exp(lam_re * step)
    lb_re, lb_im = mag * jnp.cos(lam_im * step), mag * jnp.sin(lam_im * step)
    den = lam_re * lam_re + lam_im * lam_im
    q_re = ((lb_re - 1.0) * lam_re + lb_im * lam_im) / den
    q_im = (lb_im * lam_re - (lb_re - 1.0) * lam_im) / den
    lanes = lambda a: a.transpose(1, 0, 2).reshape(ng, 2 * npar)
    rows = jnp.stack([lanes(a) for a in (lb_re, lb_im, q_re, q_im)], axis=1)
    rows = jnp.pad(rows, ((0, 0), (0, 4), (0, 0)))
    ci = jnp.arange(nc)[:, None, None]
    co = 2 * jnp.arange(half)[None, :, None] + jnp.arange(2)[None, None, :]
    dsel = (ci == co).astype(F32).reshape(nc * half, 2)
    dcol = jnp.repeat(d.astype(F32).reshape(ng, nc), half, axis=1)[..., None] * dsel[None]
    cols = jnp.concatenate([rows[:, :2].transpose(0, 2, 1), dcol], axis=2)
    perm = jnp.concatenate([jnp.arange(0, nc, 2), jnp.arange(1, nc, 2)])
    bc = jnp.stack([b_re.astype(F32).transpose(1, 3, 0, 2).reshape(ng, nc, 2 * npar),
                    b_im.astype(F32).transpose(1, 3, 0, 2).reshape(ng, nc, 2 * npar),
                    c_re.astype(F32).transpose(1, 2, 0, 3).reshape(ng, nc, 2 * npar)[:, perm],
                    c_im.astype(F32).transpose(1, 2, 0, 3).reshape(ng, nc, 2 * npar)[:, perm]],
                   axis=1)
    return rows, cols, bc


def _merge_kernel(yt_ref, ga_ref, pb_ref, x_ref, wglut_ref, wpa_ref, wout_ref, h_ref):
    def body(r0, sub):
        rows = pl.ds(r0, TOKEN_SUB)
        ncs = TOKEN_SUB // CHUNK
        y = jax.nn.gelu(jnp.concatenate([yt_ref[:, sub * ncs + c, :] for c in range(ncs)], axis=1))
        gl = _dot(wglut_ref[...], y.astype(BF16))
        ya = (y * jax.nn.sigmoid(gl)).T.astype(BF16)
        merged = ga_ref[rows, :].astype(F32) * _dot(ya, wpa_ref[...]) + pb_ref[rows, :].astype(F32)
        h_ref[rows, :] = x_ref[rows, :] + _dot(merged.astype(BF16), wout_ref[...])

    _sub_loop(x_ref.shape[0], body)


def _merge(yt3, ga, pb, x2, wglut, wpa, wout):
    t, d = x2.shape
    sw = yt3.shape[0]
    tm = TOKEN_BLOCK
    return pl.pallas_call(
        _merge_kernel,
        grid=(t // tm,),
        in_specs=[pl.BlockSpec((sw, tm // CHUNK, CHUNK), lambda i: (0, i, 0)),
                  pl.BlockSpec((tm, d), lambda i: (i, 0)),
                  pl.BlockSpec((tm, d), lambda i: (i, 0)),
                  pl.BlockSpec((tm, d), lambda i: (i, 0)),
                  _const_spec(wglut.shape), _const_spec(wpa.shape), _const_spec(wout.shape)],
        out_specs=pl.BlockSpec((tm, d), lambda i: (i, 0)),
        out_shape=jax.ShapeDtypeStruct((t, d), F32),
        compiler_params=_cparams("arbitrary"),
        name="merge",
    )(yt3, ga, pb, x2, wglut, wpa, wout)


def _xattn_kernel(h_ref, g_ref, wq_ref, k_ref, v_ref, wo_ref, o_ref):
    def body(r0, sub):
        rows = pl.ds(r0, TOKEN_SUB)
        h = h_ref[0, rows, :]
        hn = _rms(h, g_ref[...]).astype(BF16)
        q = _dot(hn, wq_ref[...]).astype(BF16)
        hd = q.shape[1] // XATTN_HEADS
        outs = []
        for hh in range(XATTN_HEADS):
            sl = slice(hh * hd, (hh + 1) * hd)
            s = _dot_nt(q[:, sl], k_ref[0, :, sl])
            e = jnp.exp(s - jnp.max(s, axis=-1, keepdims=True))
            p = e * (1.0 / jnp.sum(e, axis=-1, keepdims=True))
            outs.append(_dot(p.astype(BF16), v_ref[0, :, sl]))
        o = jnp.concatenate(outs, axis=1).astype(BF16)
        o_ref[0, rows, :] = h + _dot(o, wo_ref[...])

    _sub_loop(h_ref.shape[1], body)


def _xattn(h3, g, wq, k, v, wo):
    b, l, d = h3.shape
    m = k.shape[1]
    tm = TOKEN_BLOCK
    return pl.pallas_call(
        _xattn_kernel,
        grid=(b, l // tm),
        in_specs=[pl.BlockSpec((1, tm, d), lambda i, j: (i, j, 0)),
                  _const_spec(g.shape), _const_spec(wq.shape),
                  pl.BlockSpec((1, m, d), lambda i, j: (i, 0, 0)),
                  pl.BlockSpec((1, m, d), lambda i, j: (i, 0, 0)),
                  _const_spec(wo.shape)],
        out_specs=pl.BlockSpec((1, tm, d), lambda i, j: (i, j, 0)),
        out_shape=jax.ShapeDtypeStruct((b, l, d), F32),
        compiler_params=_cparams("arbitrary", "arbitrary"),
        name="xattn",
    )(h3, g, wq, k, v, wo)


def _ffn_kernel(h_ref, g_ref, wg_ref, wu_ref, wd_ref, gf_ref, o_ref):
    h = h_ref[...]
    hn = _rms(h, g_ref[...]).astype(BF16)
    act = (jax.nn.silu(_dot(hn, wg_ref[...])) * _dot(hn, wu_ref[...])).astype(BF16)
    o_ref[...] = _rms(h + _dot(act, wd_ref[...]), gf_ref[...])


def _ffn(h2, g, wg, wu, wd, gf, tm):
    t, d = h2.shape
    consts = (g, wg, wu, wd, gf)
    return pl.pallas_call(
        _ffn_kernel,
        grid=(t // tm,),
        in_specs=[pl.BlockSpec((tm, d), lambda i: (i, 0))] + [_const_spec(c.shape) for c in consts],
        out_specs=pl.BlockSpec((tm, d), lambda i: (i, 0)),
        out_shape=jax.ShapeDtypeStruct((t, d), F32),
        compiler_params=_cparams("arbitrary"),
        name="ffn",
    )(h2, *consts)


def _layer(h, kmem, vmem, mix_g, w_in, s5p, w_glu, ln_g, ln_b, sgu_w, sgu_bias, wpa, wpb, wout,
           xg, wq, wxo, fg, wgate, wup, wdown, final_g):
    b, l, d = h.shape
    t = b * l
    sw = w_glu.shape[0]
    su = ln_g.shape[0]
    off_u, off_v, off_ga, off_gb = sw, sw + su, sw + 2 * su, sw + 2 * su + d
    row = lambda a: a.astype(F32).reshape(1, -1)
    x2 = h.reshape(t, d)
    bias_full = jnp.repeat(sgu_bias.astype(F32).T, su // SGU_HEADS, axis=1)
    xat3, ga, pb = _in_proj(
        x2, row(mix_g), w_in[:, :off_u].T.astype(BF16), w_in[:, off_u:off_v].astype(BF16),
        w_in[:, off_v:off_ga].astype(BF16), w_in[:, off_ga:off_gb].astype(BF16),
        w_in[:, off_gb:].astype(BF16), row(ln_g), row(ln_b), sgu_w.astype(BF16),
        bias_full, wpb.astype(BF16))
    yt3 = _s5(xat3, _s5_tables(*s5p), b)
    h1 = _merge(yt3, ga, pb, x2, w_glu.T.astype(BF16), wpa.astype(BF16), wout.astype(BF16))
    h2 = _xattn(h1.reshape(b, l, d), row(xg), wq.astype(BF16), kmem, vmem, wxo.astype(BF16))
    out = _ffn(h2.reshape(t, d), row(fg), wgate.astype(BF16), wup.astype(BF16), wdown.astype(BF16),
               row(final_g), 256)
    return out.reshape(b, l, d)


def kernel(x, mem, mix_norm_g, w_in, s5_lam_re, s5_lam_im, s5_log_step, s5_b_re, s5_b_im, s5_c_re, s5_c_im, s5_d, s5_w_glu, sgu_ln_g, sgu_ln_b, sgu_w, sgu_bias, w_proj_a, w_proj_b, w_out, xattn_norm_g, mem_norm_g, w_q, w_k, w_v, w_xo, ffn_norm_g, w_gate, w_up, w_down, final_norm_g):
    depth = w_in.shape[0]
    assert depth == 1, "the final rms_norm is fused into the last layer's ffn kernel"
    hd = x.shape[-1] // XATTN_HEADS
    i = 0
    kmem, vmem = _mem_kv(mem, mem_norm_g.astype(F32).reshape(1, -1), w_k[i].astype(BF16),
                         w_v[i].astype(BF16), hd ** -0.5)
    s5p = (s5_lam_re[i], s5_lam_im[i], s5_log_step[i], s5_b_re[i], s5_b_im[i], s5_c_re[i],
           s5_c_im[i], s5_d[i])
    return _layer(x, kmem, vmem, mix_norm_g[i], w_in[i], s5p, s5_w_glu[i], sgu_ln_g[i], sgu_ln_b[i],
                  sgu_w[i], sgu_bias[i], w_proj_a[i], w_proj_b[i], w_out[i], xattn_norm_g[i], w_q[i],
                  w_xo[i], ffn_norm_g[i], w_gate[i], w_up[i], w_down[i], final_norm_g)
```

```python
import functools
from typing import NamedTuple

import jax
import jax.numpy as jnp
from jax import lax
from jax.experimental import pallas as pl
from jax.experimental.pallas import tpu as pltpu

F32 = jnp.float32
BF16 = jnp.bfloat16

EPS = 1e-6
LANES = 128
CHUNK = 128
S5_GROUP = 16
S5_STATE = 64
SGU_HEADS = 8
XATTN_HEADS = 4
VMEM_LIMIT = 56 * 1024 * 1024
TOKEN_BLOCK = 1024
TOKEN_SUB = 256


def _cparams(*sem):
    return pltpu.CompilerParams(dimension_semantics=sem, vmem_limit_bytes=VMEM_LIMIT)


def _const_spec(shape):
    nd = len(shape)
    return pl.BlockSpec(shape, lambda *_: (0,) * nd, pipeline_mode=pl.Buffered(1))


def _rms(x, g):
    return x * lax.rsqrt(jnp.mean(x * x, axis=-1, keepdims=True) + EPS) * g


def _dot(a, b):
    return jnp.dot(a, b, preferred_element_type=F32)


def _dot_nt(a, b):
    return lax.dot_general(a, b, (((1,), (1,)), ((), ())), preferred_element_type=F32)


def _sub_loop(n_tokens, body, unroll=2):
    def step(sub, carry):
        body(pl.multiple_of(sub * TOKEN_SUB, TOKEN_SUB), sub)
        return carry
    lax.fori_loop(0, n_tokens // TOKEN_SUB, step, 0, unroll=unroll)


def _mem_kv_kernel(mem_ref, g_ref, wk_ref, wv_ref, k_ref, v_ref, *, scale):
    mn = _rms(mem_ref[0], g_ref[...]).astype(BF16)
    k_ref[0] = (_dot(mn, wk_ref[...]) * scale).astype(BF16)
    v_ref[0] = _dot(mn, wv_ref[...]).astype(BF16)


def _mem_kv(mem, g, wk, wv, scale):
    b, m, d = mem.shape
    return pl.pallas_call(
        functools.partial(_mem_kv_kernel, scale=scale),
        grid=(b,),
        in_specs=[pl.BlockSpec((1, m, d), lambda i: (i, 0, 0)),
                  _const_spec((1, d)), _const_spec((d, d)), _const_spec((d, d))],
        out_specs=[pl.BlockSpec((1, m, d), lambda i: (i, 0, 0))] * 2,
        out_shape=[jax.ShapeDtypeStruct((b, m, d), BF16)] * 2,
        compiler_params=_cparams("arbitrary"),
        name="mem_kv",
    )(mem, g, wk, wv)


def _in_proj_kernel(x_ref, g_ref, wat_ref, wr_ref, lng_ref, lnb_ref,
                    ws_ref, bias_ref, wpb_ref, xat_ref, ga_ref, pb_ref):
    def body(r0, sub):
        rows = pl.ds(r0, TOKEN_SUB)
        n = _rms(x_ref[rows, :], g_ref[...]).astype(BF16)
        xat = _dot_nt(wat_ref[...], n)
        for c in range(TOKEN_SUB // CHUNK):
            xat_ref[:, sub * (TOKEN_SUB // CHUNK) + c, :] = xat[:, c * CHUNK:(c + 1) * CHUNK]
        su = lng_ref.shape[1]
        proj = _dot(n, wr_ref[...])
        ga_ref[rows, :] = jax.nn.sigmoid(proj[:, 2 * su:2 * su + ga_ref.shape[1]]).astype(BF16)
        zv = jax.nn.gelu(proj[:, su:2 * su])
        zc = zv - jnp.mean(zv, axis=-1, keepdims=True)
        zvn = (zc * lax.rsqrt(jnp.mean(zc * zc, axis=-1, keepdims=True) + EPS) * lng_ref[...]
               + lnb_ref[...]).astype(BF16)
        hd = zvn.shape[1] // SGU_HEADS
        ncs = TOKEN_SUB // CHUNK
        mixed = [_dot(ws_ref[h], jnp.concatenate(
            [zvn[c * CHUNK:(c + 1) * CHUNK, h * hd:(h + 1) * hd] for c in range(ncs)], axis=1))
            for h in range(SGU_HEADS)]
        sv = jnp.concatenate(
            [jnp.concatenate([m[:, c * hd:(c + 1) * hd] for m in mixed], axis=1) + bias_ref[...]
             for c in range(ncs)], axis=0)
        zu = jax.nn.gelu(proj[:, :su])
        yb = (zu * sv).astype(BF16)
        gb = jax.nn.sigmoid(proj[:, 2 * su + ga_ref.shape[1]:])
        pb_ref[rows, :] = (gb * _dot(yb, wpb_ref[...])).astype(BF16)

    _sub_loop(x_ref.shape[0], body)


def _in_proj(x2, g, wat, wrest, lng, lnb, ws, bias_full, wpb):
    t, d = x2.shape
    sw = wat.shape[0]
    tm = TOKEN_BLOCK
    consts = (g, wat, wrest, lng, lnb, ws, bias_full, wpb)
    return pl.pallas_call(
        _in_proj_kernel,
        grid=(t // tm,),
        in_specs=[pl.BlockSpec((tm, d), lambda i: (i, 0))] + [_const_spec(c.shape) for c in consts],
        out_specs=[pl.BlockSpec((sw, tm // CHUNK, CHUNK), lambda i: (0, i, 0)),
                   pl.BlockSpec((tm, d), lambda i: (i, 0)),
                   pl.BlockSpec((tm, d), lambda i: (i, 0))],
        out_shape=[jax.ShapeDtypeStruct((sw, t // CHUNK, CHUNK), F32),
                   jax.ShapeDtypeStruct((t, d), BF16),
                   jax.ShapeDtypeStruct((t, d), BF16)],
        compiler_params=_cparams("arbitrary"),
        name="in_proj",
    )(x2, *consts)


def _cmul(ar, ai, br, bi):
    return ar * br - ai * bi, ar * bi + ai * br


def _cpow(br, bi, e, nbits):
    rr = ri = None
    for k in range(nbits):
        bit = ((e >> k) & 1) == 1
        sr, si = jnp.where(bit, br, 1.0), jnp.where(bit, bi, 0.0)
        rr, ri = (sr, si) if rr is None else _cmul(rr, ri, sr, si)
        if k + 1 < nbits:
            br, bi = _cmul(br, bi, br, bi)
    return rr, ri


class _S5Tabs(NamedTuple):
    kf: tuple
    kb: tuple
    lt: tuple
    et: tuple
    bb: tuple
    cc: tuple
    a: tuple


def _s5_tables_in_kernel(row_ref, col_ref, bc_ref):
    gc, half, ns = S5_GROUP, S5_GROUP // 2, 2 * S5_STATE
    sq = (CHUNK, CHUNK)
    sub = lax.broadcasted_iota(jnp.int32, sq, 0)
    lan = lax.broadcasted_iota(jnp.int32, sq, 1)
    lb_re, lb_im = row_ref[0, 0:1, :], row_ref[0, 1:2, :]
    q_re, q_im = row_ref[0, 2:3, :], row_ref[0, 3:4, :]
    fwd_lane = lan < S5_STATE
    lt = _cpow(lb_re, lb_im, jnp.where(fwd_lane, CHUNK - 1 - sub, sub), 7)
    et = _cpow(lb_re, lb_im, jnp.where(fwd_lane, sub + 1, CHUNK - sub), 8)
    p_re, p_im = _cpow(col_ref[0, :, 0:1], col_ref[0, :, 1:2],
                       jnp.where(sub < S5_STATE, lan, (CHUNK - lan) & (CHUNK - 1)), 7)
    a_re, a_im = lb_re, lb_im
    for _ in range(7):
        a_re, a_im = _cmul(a_re, a_im, a_re, a_im)
    bb = _cmul(q_re, q_im, bc_ref[0, 0], bc_ref[0, 1])
    c_re, c_im = bc_ref[0, 2], bc_ref[0, 3]

    rhs_f = jnp.concatenate([p_re[:S5_STATE], p_im[:S5_STATE]], axis=0)
    rhs_b = jnp.concatenate([p_re[S5_STATE:], p_im[S5_STATE:]], axis=0)
    lane_f = lax.broadcasted_iota(jnp.int32, (gc * half, ns), 1) < S5_STATE
    lane_tap = lax.broadcasted_iota(jnp.int32, (gc * half, CHUNK), 1)
    taps = []
    for par in range(2):
        cpr, cpi = c_re[par * half:(par + 1) * half], c_im[par * half:(par + 1) * half]
        cb = [_cmul(cpr, cpi, bb[0][ci:ci + 1], bb[1][ci:ci + 1]) for ci in range(gc)]
        cb_re = jnp.concatenate([v[0] for v in cb], axis=0)
        cb_im = jnp.concatenate([v[1] for v in cb], axis=0)
        lhs_f = jnp.where(lane_f, cb_re, -pltpu.roll(cb_im, S5_STATE, 1))
        lhs_b = jnp.where(lane_f, pltpu.roll(cb_re, S5_STATE, 1), -cb_im)
        kf = jnp.dot(lhs_f, rhs_f, preferred_element_type=F32, precision=lax.Precision.HIGHEST)
        kb = jnp.dot(lhs_b, rhs_b, preferred_element_type=F32, precision=lax.Precision.HIGHEST)
        kf = kf + jnp.where(lane_tap == 0, kb[:, 0:1] + col_ref[0, :, 2 + par:3 + par], 0.0)
        taps.append((kf, kb))

    return _S5Tabs((taps[0][0], taps[1][0]), (taps[0][1], taps[1][1]),
                   lt, et, bb, (c_re, c_im), (a_re, a_im))


def _s5_build_small(tabs, slot, bm_ref, cm_ref, a_ref):
    half = S5_GROUP // 2
    for ch in range(S5_GROUP):
        w_re, w_im = _cmul(*tabs.lt, tabs.bb[0][ch:ch + 1], tabs.bb[1][ch:ch + 1])
        bm_ref[slot, ch * CHUNK:(ch + 1) * CHUNK, :] = jnp.concatenate([w_re, w_im], axis=1).astype(BF16)
        r = (ch % 2) * half + ch // 2
        g_re, g_im = _cmul(*tabs.et, tabs.cc[0][r:r + 1], tabs.cc[1][r:r + 1])
        cm_ref[slot, ch * CHUNK:(ch + 1) * CHUNK, :] = jnp.concatenate([g_re, -g_im], axis=1).astype(BF16)
    a_ref[slot, 0] = jnp.broadcast_to(tabs.a[0], a_ref.shape[2:])
    a_ref[slot, 1] = jnp.broadcast_to(tabs.a[1], a_ref.shape[2:])


def _s5_build_pair(tabs, slot, q, m_ref):
    half = S5_GROUP // 2
    sq = (CHUNK, CHUNK)
    upper = lax.broadcasted_iota(jnp.int32, sq, 1) >= lax.broadcasted_iota(jnp.int32, sq, 0)
    for ci in range(S5_GROUP):
        r = ci * half + q
        blocks = []
        for par in range(2):
            vf = jnp.broadcast_to(tabs.kf[par][r:r + 1], sq)
            vb = jnp.broadcast_to(tabs.kb[par][r:r + 1], sq)
            cf = pltpu.roll(vf, 0, 1, stride=1, stride_axis=0)
            cbw = pltpu.roll(vb, 0, 1, stride=1, stride_axis=0)
            blocks.append(jnp.where(upper, cf, cbw))
        m_ref[slot, ci * CHUNK:(ci + 1) * CHUNK, 2 * q * CHUNK:(2 * q + 2) * CHUNK] = (
            jnp.concatenate(blocks, axis=1).astype(BF16))


def _s5_apply(x_ref, y_ref, gi, slot, m_ref, bm_ref, cm_ref, a_ref, s_ref, xf_ref, xb_ref, nbatch,
              between):
    gc, ns = S5_GROUP, 2 * S5_STATE
    nrows = x_ref.shape[1]
    nch = nrows // nbatch
    x = jnp.concatenate([x_ref[gi * gc + ci].astype(BF16) for ci in range(gc)], axis=1)
    s = _dot(x, bm_ref[slot])
    s_ref[0] = s[:, :ns]
    s_ref[1] = s[:, ns:]
    a_re, a_im = a_ref[slot, 0, 0:1, :], a_ref[slot, 1, 0:1, :]
    is_fwd = lax.broadcasted_iota(jnp.int32, (nbatch, ns), 1) < S5_STATE
    st_re = st_im = jnp.zeros((nbatch, ns), F32)
    for i in range(nch):
        rows_f = pl.ds(i, nbatch, stride=nch)
        rows_b = pl.ds(nch - 1 - i, nbatch, stride=nch)
        xf_ref[0, rows_f, :] = st_re
        xf_ref[1, rows_f, :] = st_im
        xb_ref[0, rows_b, :] = st_re
        xb_ref[1, rows_b, :] = st_im
        if i + 1 < nch:
            s_re = jnp.where(is_fwd, s_ref[0, rows_f, :], s_ref[0, rows_b, :])
            s_im = jnp.where(is_fwd, s_ref[1, rows_f, :], s_ref[1, rows_b, :])
            n_re, n_im = _cmul(a_re, a_im, st_re, st_im)
            st_re, st_im = n_re + s_re, n_im + s_im
    is_fwd_all = lax.broadcasted_iota(jnp.int32, (nrows, ns), 1) < S5_STATE
    xin = jnp.concatenate([jnp.where(is_fwd_all, xf_ref[0], xb_ref[0]),
                           jnp.where(is_fwd_all, xf_ref[1], xb_ref[1])], axis=1).astype(BF16)
    for q in range(gc // 2):
        between(q)
        cols = slice(2 * q * CHUNK, (2 * q + 2) * CHUNK)
        y = _dot(x, m_ref[slot, :, cols]) + _dot_nt(xin, cm_ref[slot, cols, :])
        y_ref[gi * gc + 2 * q] = y[:, :CHUNK]
        y_ref[gi * gc + 2 * q + 1] = y[:, CHUNK:]


def _s5_kernel(x_ref, row0, col0, bc0, row1, col1, bc1,
               y_ref, m_ref, bm_ref, cm_ref, a_ref, s_ref, xf_ref, xb_ref, *, nbatch):
    scr = (m_ref, bm_ref, cm_ref, a_ref, s_ref, xf_ref, xb_ref)
    k = pl.program_id(0)

    @pl.when(k == 0)
    def _():
        tabs = _s5_tables_in_kernel(row0, col0, bc0)
        _s5_build_small(tabs, 0, bm_ref, cm_ref, a_ref)
        for q in range(S5_GROUP // 2):
            _s5_build_pair(tabs, 0, q, m_ref)

    for cur in range(2):
        @pl.when(k % 2 == cur)
        def _():
            tabs = _s5_tables_in_kernel(row1, col1, bc1)
            _s5_build_small(tabs, 1 - cur, bm_ref, cm_ref, a_ref)
            _s5_apply(x_ref, y_ref, 0, cur, *scr, nbatch,
                      functools.partial(_s5_build_pair, tabs, 1 - cur, m_ref=m_ref))


def _s5(xat3, tabs, nbatch):
    sw, nrows, _ = xat3.shape
    gc, ns = S5_GROUP, 2 * S5_STATE
    nsteps = sw // gc
    last = nsteps - 1

    def tspecs(index):
        return [pl.BlockSpec((1,) + a.shape[1:], lambda k, n=a.ndim: (index(k),) + (0,) * (n - 1))
                for a in tabs]

    return pl.pallas_call(
        functools.partial(_s5_kernel, nbatch=nbatch),
        grid=(nsteps,),
        in_specs=[pl.BlockSpec((gc, nrows, CHUNK), lambda k: (k, 0, 0))]
                 + tspecs(lambda k: 0) + tspecs(lambda k: jnp.minimum(k + 1, last)),
        out_specs=pl.BlockSpec((gc, nrows, CHUNK), lambda k: (k, 0, 0)),
        out_shape=jax.ShapeDtypeStruct((sw, nrows, CHUNK), F32),
        scratch_shapes=[pltpu.VMEM((2, gc * CHUNK, gc * CHUNK), BF16),
                        pltpu.VMEM((2, gc * CHUNK, 2 * ns), BF16),
                        pltpu.VMEM((2, gc * CHUNK, 2 * ns), BF16),
                        pltpu.VMEM((2, 2, 8, ns), F32),
                        pltpu.VMEM((2, nrows, ns), F32),
                        pltpu.VMEM((2, nrows, ns), F32),
                        pltpu.VMEM((2, nrows, ns), F32)],
        compiler_params=_cparams("arbitrary"),
        name="s5",
    )(xat3, *tabs, *tabs)


def _s5_tables(lam_re, lam_im, log_step, b_re, b_im, c_re, c_im, d):
    ng, npar, nc = b_re.shape[1], b_re.shape[2], b_re.shape[3]
    half = nc // 2
    lam_re, lam_im = lam_re.astype(F32), lam_im.astype(F32)
    step = jnp.exp(log_step.astype(F32))[..., None]
    mag = jnp.exp(lam_re * step)
    lb_re, lb_im = mag * jnp.cos(lam_im * step), mag * jnp.sin(lam_im * step)
    den = lam_re * lam_re + lam_im * lam_im
    q_re = ((lb_re - 1.0) * lam_re + lb_im * lam_im) / den
    q_im = (lb_im * lam_re - (lb_re - 1.0) * lam_im) / den
    lanes = lambda a: a.transpose(1, 0, 2).reshape(ng, 2 * npar)
    rows = jnp.stack([lanes(a) for a in (lb_re, lb_im, q_re, q_im)], axis=1)
    rows = jnp.pad(rows, ((0, 0), (0, 4), (0, 0)))
    ci = jnp.arange(nc)[:, None, None]
    co = 2 * jnp.arange(half)[None, :, None] + jnp.arange(2)[None, None, :]
    dsel = (ci == co).astype(F32).reshape(nc * half, 2)
    dcol = jnp.repeat(d.astype(F32).reshape(ng, nc), half, axis=1)[..., None] * dsel[None]
    cols = jnp.concatenate([rows[:, :2].transpose(0, 2, 1), dcol], axis=2)
    perm = jnp.concatenate([jnp.arange(0, nc, 2), jnp.arange(1, nc, 2)])
    bc = jnp.stack([b_re.astype(F32).transpose(1, 3, 0, 2).reshape(ng, nc, 2 * npar),
                    b_im.astype(F32).transpose(1, 3, 0, 2).reshape(ng, nc, 2 * npar),
                    c_re.astype(F32).transpose(1, 2, 0, 3).reshape(ng, nc, 2 * npar)[:, perm],
                    c_im.astype(F32).transpose(1, 2, 0, 3).reshape(ng, nc, 2 * npar)[:, perm]],
                   axis=1)
    return rows, cols, bc


def _merge_kernel(yt_ref, ga_ref, pb_ref, x_ref, wglut_ref, wpa_ref, wout_ref, h_ref):
    def body(r0, sub):
        rows = pl.ds(r0, TOKEN_SUB)
        ncs = TOKEN_SUB // CHUNK
        y = jax.nn.gelu(jnp.concatenate([yt_ref[:, sub * ncs + c, :] for c in range(ncs)], axis=1))
        gl = _dot(wglut_ref[...], y.astype(BF16))
        ya = (y * jax.nn.sigmoid(gl)).T.astype(BF16)
        merged = ga_ref[rows, :].astype(F32) * _dot(ya, wpa_ref[...]) + pb_ref[rows, :].astype(F32)
        h_ref[rows, :] = x_ref[rows, :] + _dot(merged.astype(BF16), wout_ref[...])

    _sub_loop(x_ref.shape[0], body, unroll=4)


def _merge(yt3, ga, pb, x2, wglut, wpa, wout):
    t, d = x2.shape
    sw = yt3.shape[0]
    tm = TOKEN_BLOCK
    return pl.pallas_call(
        _merge_kernel,
        grid=(t // tm,),
        in_specs=[pl.BlockSpec((sw, tm // CHUNK, CHUNK), lambda i: (0, i, 0)),
                  pl.BlockSpec((tm, d), lambda i: (i, 0)),
                  pl.BlockSpec((tm, d), lambda i: (i, 0)),
                  pl.BlockSpec((tm, d), lambda i: (i, 0)),
                  _const_spec(wglut.shape), _const_spec(wpa.shape), _const_spec(wout.shape)],
        out_specs=pl.BlockSpec((tm, d), lambda i: (i, 0)),
        out_shape=jax.ShapeDtypeStruct((t, d), F32),
        compiler_params=_cparams("arbitrary"),
        name="merge",
    )(yt3, ga, pb, x2, wglut, wpa, wout)


def _xattn_kernel(h_ref, g_ref, wq_ref, k_ref, v_ref, wo_ref, o_ref):
    def body(r0, sub):
        rows = pl.ds(r0, TOKEN_SUB)
        h = h_ref[0, rows, :]
        hn = _rms(h, g_ref[...]).astype(BF16)
        q = _dot(hn, wq_ref[...]).astype(BF16)
        hd = q.shape[1] // XATTN_HEADS
        outs = []
        for hh in range(XATTN_HEADS):
            sl = slice(hh * hd, (hh + 1) * hd)
            s = _dot_nt(q[:, sl], k_ref[0, :, sl])
            e = jnp.exp(s - jnp.max(s, axis=-1, keepdims=True))
            p = e * (1.0 / jnp.sum(e, axis=-1, keepdims=True))
            outs.append(_dot(p.astype(BF16), v_ref[0, :, sl]))
        o = jnp.concatenate(outs, axis=1).astype(BF16)
        o_ref[0, rows, :] = h + _dot(o, wo_ref[...])

    _sub_loop(h_ref.shape[1], body, unroll=4)


def _xattn(h3, g, wq, k, v, wo):
    b, l, d = h3.shape
    m = k.shape[1]
    tm = TOKEN_BLOCK
    return pl.pallas_call(
        _xattn_kernel,
        grid=(b, l // tm),
        in_specs=[pl.BlockSpec((1, tm, d), lambda i, j: (i, j, 0)),
                  _const_spec(g.shape), _const_spec(wq.shape),
                  pl.BlockSpec((1, m, d), lambda i, j: (i, 0, 0)),
                  pl.BlockSpec((1, m, d), lambda i, j: (i, 0, 0)),
                  _const_spec(wo.shape)],
        out_specs=pl.BlockSpec((1, tm, d), lambda i, j: (i, j, 0)),
        out_shape=jax.ShapeDtypeStruct((b, l, d), F32),
        compiler_params=_cparams("arbitrary", "arbitrary"),
        name="xattn",
    )(h3, g, wq, k, v, wo)


def _ffn_kernel(h_ref, g_ref, wg_ref, wu_ref, wd_ref, gf_ref, o_ref):
    def body(r0, sub):
        rows = pl.ds(r0, TOKEN_SUB)
        h = h_ref[rows, :]
        hn = _rms(h, g_ref[...]).astype(BF16)
        act = (jax.nn.silu(_dot(hn, wg_ref[...])) * _dot(hn, wu_ref[...])).astype(BF16)
        o_ref[rows, :] = _rms(h + _dot(act, wd_ref[...]), gf_ref[...])

    _sub_loop(h_ref.shape[0], body)


def _ffn(h2, g, wg, wu, wd, gf):
    t, d = h2.shape
    tm = TOKEN_BLOCK
    consts = (g, wg, wu, wd, gf)
    return pl.pallas_call(
        _ffn_kernel,
        grid=(t // tm,),
        in_specs=[pl.BlockSpec((tm, d), lambda i: (i, 0))] + [_const_spec(c.shape) for c in consts],
        out_specs=pl.BlockSpec((tm, d), lambda i: (i, 0)),
        out_shape=jax.ShapeDtypeStruct((t, d), F32),
        compiler_params=_cparams("arbitrary"),
        name="ffn",
    )(h2, *consts)


def _layer(h, kmem, vmem, mix_g, w_in, s5p, w_glu, ln_g, ln_b, sgu_w, sgu_bias, wpa, wpb, wout,
           xg, wq, wxo, fg, wgate, wup, wdown, final_g):
    b, l, d = h.shape
    t = b * l
    sw = w_glu.shape[0]
    su = ln_g.shape[0]
    row = lambda a: a.astype(F32).reshape(1, -1)
    x2 = h.reshape(t, d)
    bias_full = jnp.repeat(sgu_bias.astype(F32).T, su // SGU_HEADS, axis=1)
    xat3, ga, pb = _in_proj(
        x2, row(mix_g), w_in[:, :sw].T.astype(BF16), w_in[:, sw:].astype(BF16),
        row(ln_g), row(ln_b), sgu_w.astype(BF16), bias_full, wpb.astype(BF16))
    yt3 = _s5(xat3, _s5_tables(*s5p), b)
    h1 = _merge(yt3, ga, pb, x2, w_glu.T.astype(BF16), wpa.astype(BF16), wout.astype(BF16))
    h2 = _xattn(h1.reshape(b, l, d), row(xg), wq.astype(BF16), kmem, vmem, wxo.astype(BF16))
    out = _ffn(h2.reshape(t, d), row(fg), wgate.astype(BF16), wup.astype(BF16), wdown.astype(BF16),
               row(final_g))
    return out.reshape(b, l, d)


def kernel(x, mem, mix_norm_g, w_in, s5_lam_re, s5_lam_im, s5_log_step, s5_b_re, s5_b_im, s5_c_re, s5_c_im, s5_d, s5_w_glu, sgu_ln_g, sgu_ln_b, sgu_w, sgu_bias, w_proj_a, w_proj_b, w_out, xattn_norm_g, mem_norm_g, w_q, w_k, w_v, w_xo, ffn_norm_g, w_gate, w_up, w_down, final_norm_g):
    depth = w_in.shape[0]
    assert depth == 1, "the final rms_norm is fused into the last layer's ffn kernel"
    hd = x.shape[-1] // XATTN_HEADS
    i = 0
    kmem, vmem = _mem_kv(mem, mem_norm_g.astype(F32).reshape(1, -1), w_k[i].astype(BF16),
                         w_v[i].astype(BF16), hd ** -0.5)
    s5p = (s5_lam_re[i], s5_lam_im[i], s5_log_step[i], s5_b_re[i], s5_b_im[i], s5_c_re[i],
           s5_c_im[i], s5_d[i])
    return _layer(x, kmem, vmem, mix_norm_g[i], w_in[i], s5p, s5_w_glu[i], sgu_ln_g[i], sgu_ln_b[i],
                  sgu_w[i], sgu_bias[i], w_proj_a[i], w_proj_b[i], w_out[i], xattn_norm_g[i], w_q[i],
                  w_xo[i], ffn_norm_g[i], w_gate[i], w_up[i], w_down[i], final_norm_g)
```

```python
import functools
from typing import NamedTuple

import jax
import jax.numpy as jnp
from jax import lax
from jax.experimental import pallas as pl
from jax.experimental.pallas import tpu as pltpu

F32 = jnp.float32
BF16 = jnp.bfloat16

EPS = 1e-6
LANES = 128
CHUNK = 128
S5_GROUP = 16
S5_STATE = 64
SGU_HEADS = 8
XATTN_HEADS = 4
VMEM_LIMIT = 56 * 1024 * 1024
TOKEN_BLOCK = 1024
TOKEN_SUB = 256
TOKEN_SUB_WIDE = 512


def _cparams(*sem):
    return pltpu.CompilerParams(dimension_semantics=sem, vmem_limit_bytes=VMEM_LIMIT)


def _const_spec(shape):
    nd = len(shape)
    return pl.BlockSpec(shape, lambda *_: (0,) * nd, pipeline_mode=pl.Buffered(1))


def _rms(x, g):
    return x * lax.rsqrt(jnp.mean(x * x, axis=-1, keepdims=True) + EPS) * g


def _dot(a, b):
    return jnp.dot(a, b, preferred_element_type=F32)


def _dot_nt(a, b):
    return lax.dot_general(a, b, (((1,), (1,)), ((), ())), preferred_element_type=F32)


def _sub_loop(n_tokens, body, ts=TOKEN_SUB):
    def step(sub, carry):
        body(pl.multiple_of(sub * ts, ts), sub)
        return carry
    lax.fori_loop(0, n_tokens // ts, step, 0, unroll=2)


def _mem_kv_kernel(mem_ref, g_ref, wk_ref, wv_ref, k_ref, v_ref, *, scale):
    mn = _rms(mem_ref[0], g_ref[...]).astype(BF16)
    k_ref[0] = (_dot(mn, wk_ref[...]) * scale).astype(BF16)
    v_ref[0] = _dot(mn, wv_ref[...]).astype(BF16)


def _mem_kv(mem, g, wk, wv, scale):
    b, m, d = mem.shape
    return pl.pallas_call(
        functools.partial(_mem_kv_kernel, scale=scale),
        grid=(b,),
        in_specs=[pl.BlockSpec((1, m, d), lambda i: (i, 0, 0)),
                  _const_spec((1, d)), _const_spec((d, d)), _const_spec((d, d))],
        out_specs=[pl.BlockSpec((1, m, d), lambda i: (i, 0, 0))] * 2,
        out_shape=[jax.ShapeDtypeStruct((b, m, d), BF16)] * 2,
        compiler_params=_cparams("arbitrary"),
        name="mem_kv",
    )(mem, g, wk, wv)


def _in_proj_kernel(x_ref, g_ref, wat_ref, wr_ref, lng_ref, lnb_ref,
                    ws_ref, bias_ref, wpb_ref, xat_ref, ga_ref, pb_ref):
    def body(r0, sub):
        rows = pl.ds(r0, TOKEN_SUB)
        n = _rms(x_ref[rows, :], g_ref[...]).astype(BF16)
        xat = _dot_nt(wat_ref[...], n)
        for c in range(TOKEN_SUB // CHUNK):
            xat_ref[:, sub * (TOKEN_SUB // CHUNK) + c, :] = xat[:, c * CHUNK:(c + 1) * CHUNK]
        su = lng_ref.shape[1]
        proj = _dot(n, wr_ref[...])
        ga_ref[rows, :] = jax.nn.sigmoid(proj[:, 2 * su:2 * su + ga_ref.shape[1]]).astype(BF16)
        zv = jax.nn.gelu(proj[:, su:2 * su])
        zc = zv - jnp.mean(zv, axis=-1, keepdims=True)
        zvn = (zc * lax.rsqrt(jnp.mean(zc * zc, axis=-1, keepdims=True) + EPS) * lng_ref[...]
               + lnb_ref[...]).astype(BF16)
        hd = zvn.shape[1] // SGU_HEADS
        ncs = TOKEN_SUB // CHUNK
        mixed = [_dot(ws_ref[h], jnp.concatenate(
            [zvn[c * CHUNK:(c + 1) * CHUNK, h * hd:(h + 1) * hd] for c in range(ncs)], axis=1))
            for h in range(SGU_HEADS)]
        sv = jnp.concatenate(
            [jnp.concatenate([m[:, c * hd:(c + 1) * hd] for m in mixed], axis=1) + bias_ref[...]
             for c in range(ncs)], axis=0)
        zu = jax.nn.gelu(proj[:, :su])
        yb = (zu * sv).astype(BF16)
        gb = jax.nn.sigmoid(proj[:, 2 * su + ga_ref.shape[1]:])
        pb_ref[rows, :] = (gb * _dot(yb, wpb_ref[...])).astype(BF16)

    _sub_loop(x_ref.shape[0], body)


def _in_proj(x2, g, wat, wrest, lng, lnb, ws, bias_full, wpb):
    t, d = x2.shape
    sw = wat.shape[0]
    tm = TOKEN_BLOCK
    consts = (g, wat, wrest, lng, lnb, ws, bias_full, wpb)
    return pl.pallas_call(
        _in_proj_kernel,
        grid=(t // tm,),
        in_specs=[pl.BlockSpec((tm, d), lambda i: (i, 0))] + [_const_spec(c.shape) for c in consts],
        out_specs=[pl.BlockSpec((sw, tm // CHUNK, CHUNK), lambda i: (0, i, 0)),
                   pl.BlockSpec((tm, d), lambda i: (i, 0)),
                   pl.BlockSpec((tm, d), lambda i: (i, 0))],
        out_shape=[jax.ShapeDtypeStruct((sw, t // CHUNK, CHUNK), F32),
                   jax.ShapeDtypeStruct((t, d), BF16),
                   jax.ShapeDtypeStruct((t, d), BF16)],
        compiler_params=_cparams("arbitrary"),
        name="in_proj",
    )(x2, *consts)


def _cmul(ar, ai, br, bi):
    return ar * br - ai * bi, ar * bi + ai * br


def _cpow(br, bi, e, nbits):
    rr = ri = None
    for k in range(nbits):
        bit = ((e >> k) & 1) == 1
        sr, si = jnp.where(bit, br, 1.0), jnp.where(bit, bi, 0.0)
        rr, ri = (sr, si) if rr is None else _cmul(rr, ri, sr, si)
        if k + 1 < nbits:
            br, bi = _cmul(br, bi, br, bi)
    return rr, ri


class _S5Tabs(NamedTuple):
    kf: tuple
    kb: tuple
    lt: tuple
    et: tuple
    bb: tuple
    cc: tuple
    a: tuple


def _s5_tables_in_kernel(row_ref, col_ref, bc_ref):
    gc, half, ns = S5_GROUP, S5_GROUP // 2, 2 * S5_STATE
    sq = (CHUNK, CHUNK)
    sub = lax.broadcasted_iota(jnp.int32, sq, 0)
    lan = lax.broadcasted_iota(jnp.int32, sq, 1)
    lb_re, lb_im = row_ref[0, 0:1, :], row_ref[0, 1:2, :]
    q_re, q_im = row_ref[0, 2:3, :], row_ref[0, 3:4, :]
    fwd_lane = lan < S5_STATE
    lt = _cpow(lb_re, lb_im, jnp.where(fwd_lane, CHUNK - 1 - sub, sub), 7)
    et = _cpow(lb_re, lb_im, jnp.where(fwd_lane, sub + 1, CHUNK - sub), 8)
    p_re, p_im = _cpow(col_ref[0, :, 0:1], col_ref[0, :, 1:2],
                       jnp.where(sub < S5_STATE, lan, (CHUNK - lan) & (CHUNK - 1)), 7)
    a_re, a_im = lb_re, lb_im
    for _ in range(7):
        a_re, a_im = _cmul(a_re, a_im, a_re, a_im)
    bb = _cmul(q_re, q_im, bc_ref[0, 0], bc_ref[0, 1])
    c_re, c_im = bc_ref[0, 2], bc_ref[0, 3]

    rhs_f = jnp.concatenate([p_re[:S5_STATE], p_im[:S5_STATE]], axis=0)
    rhs_b = jnp.concatenate([p_re[S5_STATE:], p_im[S5_STATE:]], axis=0)
    lane_f = lax.broadcasted_iota(jnp.int32, (gc * half, ns), 1) < S5_STATE
    lane_tap = lax.broadcasted_iota(jnp.int32, (gc * half, CHUNK), 1)
    taps = []
    for par in range(2):
        cpr, cpi = c_re[par * half:(par + 1) * half], c_im[par * half:(par + 1) * half]
        cb = [_cmul(cpr, cpi, bb[0][ci:ci + 1], bb[1][ci:ci + 1]) for ci in range(gc)]
        cb_re = jnp.concatenate([v[0] for v in cb], axis=0)
        cb_im = jnp.concatenate([v[1] for v in cb], axis=0)
        lhs_f = jnp.where(lane_f, cb_re, -pltpu.roll(cb_im, S5_STATE, 1))
        lhs_b = jnp.where(lane_f, pltpu.roll(cb_re, S5_STATE, 1), -cb_im)
        kf = jnp.dot(lhs_f, rhs_f, preferred_element_type=F32, precision=lax.Precision.HIGHEST)
        kb = jnp.dot(lhs_b, rhs_b, preferred_element_type=F32, precision=lax.Precision.HIGHEST)
        kf = kf + jnp.where(lane_tap == 0, kb[:, 0:1] + col_ref[0, :, 2 + par:3 + par], 0.0)
        taps.append((kf, kb))

    return _S5Tabs((taps[0][0], taps[1][0]), (taps[0][1], taps[1][1]),
                   lt, et, bb, (c_re, c_im), (a_re, a_im))


def _s5_build_small(tabs, slot, bm_ref, cm_ref, a_ref):
    half = S5_GROUP // 2
    for ch in range(S5_GROUP):
        w_re, w_im = _cmul(*tabs.lt, tabs.bb[0][ch:ch + 1], tabs.bb[1][ch:ch + 1])
        bm_ref[slot, ch * CHUNK:(ch + 1) * CHUNK, :] = jnp.concatenate([w_re, w_im], axis=1).astype(BF16)
        r = (ch % 2) * half + ch // 2
        g_re, g_im = _cmul(*tabs.et, tabs.cc[0][r:r + 1], tabs.cc[1][r:r + 1])
        cm_ref[slot, ch * CHUNK:(ch + 1) * CHUNK, :] = jnp.concatenate([g_re, -g_im], axis=1).astype(BF16)
    a_ref[slot, 0] = jnp.broadcast_to(tabs.a[0], a_ref.shape[2:])
    a_ref[slot, 1] = jnp.broadcast_to(tabs.a[1], a_ref.shape[2:])


def _s5_build_pair(tabs, slot, q, m_ref):
    half = S5_GROUP // 2
    sq = (CHUNK, CHUNK)
    upper = lax.broadcasted_iota(jnp.int32, sq, 1) >= lax.broadcasted_iota(jnp.int32, sq, 0)
    for ci in range(S5_GROUP):
        r = ci * half + q
        blocks = []
        for par in range(2):
            vf = jnp.broadcast_to(tabs.kf[par][r:r + 1], sq)
            vb = jnp.broadcast_to(tabs.kb[par][r:r + 1], sq)
            cf = pltpu.roll(vf, 0, 1, stride=1, stride_axis=0)
            cbw = pltpu.roll(vb, 0, 1, stride=1, stride_axis=0)
            blocks.append(jnp.where(upper, cf, cbw))
        m_ref[slot, ci * CHUNK:(ci + 1) * CHUNK, 2 * q * CHUNK:(2 * q + 2) * CHUNK] = (
            jnp.concatenate(blocks, axis=1).astype(BF16))


def _s5_apply(x_ref, y_ref, gi, slot, m_ref, bm_ref, cm_ref, a_ref, s_ref, xf_ref, xb_ref, nbatch,
              between):
    gc, ns = S5_GROUP, 2 * S5_STATE
    nrows = x_ref.shape[1]
    nch = nrows // nbatch
    x = jnp.concatenate([x_ref[gi * gc + ci].astype(BF16) for ci in range(gc)], axis=1)
    s = _dot(x, bm_ref[slot])
    s_ref[0] = s[:, :ns]
    s_ref[1] = s[:, ns:]
    a_re, a_im = a_ref[slot, 0, 0:1, :], a_ref[slot, 1, 0:1, :]
    is_fwd = lax.broadcasted_iota(jnp.int32, (nbatch, ns), 1) < S5_STATE
    st_re = st_im = jnp.zeros((nbatch, ns), F32)
    for i in range(nch):
        rows_f = pl.ds(i, nbatch, stride=nch)
        rows_b = pl.ds(nch - 1 - i, nbatch, stride=nch)
        xf_ref[0, rows_f, :] = st_re
        xf_ref[1, rows_f, :] = st_im
        xb_ref[0, rows_b, :] = st_re
        xb_ref[1, rows_b, :] = st_im
        if i + 1 < nch:
            s_re = jnp.where(is_fwd, s_ref[0, rows_f, :], s_ref[0, rows_b, :])
            s_im = jnp.where(is_fwd, s_ref[1, rows_f, :], s_ref[1, rows_b, :])
            n_re, n_im = _cmul(a_re, a_im, st_re, st_im)
            st_re, st_im = n_re + s_re, n_im + s_im
    is_fwd_all = lax.broadcasted_iota(jnp.int32, (nrows, ns), 1) < S5_STATE
    xin = jnp.concatenate([jnp.where(is_fwd_all, xf_ref[0], xb_ref[0]),
                           jnp.where(is_fwd_all, xf_ref[1], xb_ref[1])], axis=1).astype(BF16)
    for q in range(gc // 2):
        between(q)
        cols = slice(2 * q * CHUNK, (2 * q + 2) * CHUNK)
        y = _dot(x, m_ref[slot, :, cols]) + _dot_nt(xin, cm_ref[slot, cols, :])
        y_ref[gi * gc + 2 * q] = y[:, :CHUNK]
        y_ref[gi * gc + 2 * q + 1] = y[:, CHUNK:]


def _s5_kernel(x_ref, row0, col0, bc0, row1, col1, bc1,
               y_ref, m_ref, bm_ref, cm_ref, a_ref, s_ref, xf_ref, xb_ref, *, nbatch):
    scr = (m_ref, bm_ref, cm_ref, a_ref, s_ref, xf_ref, xb_ref)
    k = pl.program_id(0)

    @pl.when(k == 0)
    def _():
        tabs = _s5_tables_in_kernel(row0, col0, bc0)
        _s5_build_small(tabs, 0, bm_ref, cm_ref, a_ref)
        for q in range(S5_GROUP // 2):
            _s5_build_pair(tabs, 0, q, m_ref)

    for cur in range(2):
        @pl.when(k % 2 == cur)
        def _():
            tabs = _s5_tables_in_kernel(row1, col1, bc1)
            _s5_build_small(tabs, 1 - cur, bm_ref, cm_ref, a_ref)
            _s5_apply(x_ref, y_ref, 0, cur, *scr, nbatch,
                      functools.partial(_s5_build_pair, tabs, 1 - cur, m_ref=m_ref))


def _s5(xat3, tabs, nbatch):
    sw, nrows, _ = xat3.shape
    gc, ns = S5_GROUP, 2 * S5_STATE
    nsteps = sw // gc
    last = nsteps - 1

    def tspecs(index):
        return [pl.BlockSpec((1,) + a.shape[1:], lambda k, n=a.ndim: (index(k),) + (0,) * (n - 1))
                for a in tabs]

    return pl.pallas_call(
        functools.partial(_s5_kernel, nbatch=nbatch),
        grid=(nsteps,),
        in_specs=[pl.BlockSpec((gc, nrows, CHUNK), lambda k: (k, 0, 0))]
                 + tspecs(lambda k: 0) + tspecs(lambda k: jnp.minimum(k + 1, last)),
        out_specs=pl.BlockSpec((gc, nrows, CHUNK), lambda k: (k, 0, 0)),
        out_shape=jax.ShapeDtypeStruct((sw, nrows, CHUNK), F32),
        scratch_shapes=[pltpu.VMEM((2, gc * CHUNK, gc * CHUNK), BF16),
                        pltpu.VMEM((2, gc * CHUNK, 2 * ns), BF16),
                        pltpu.VMEM((2, gc * CHUNK, 2 * ns), BF16),
                        pltpu.VMEM((2, 2, 8, ns), F32),
                        pltpu.VMEM((2, nrows, ns), F32),
                        pltpu.VMEM((2, nrows, ns), F32),
                        pltpu.VMEM((2, nrows, ns), F32)],
        compiler_params=_cparams("arbitrary"),
        name="s5",
    )(xat3, *tabs, *tabs)


def _s5_tables(lam_re, lam_im, log_step, b_re, b_im, c_re, c_im, d):
    ng, npar, nc = b_re.shape[1], b_re.shape[2], b_re.shape[3]
    half = nc // 2
    lam_re, lam_im = lam_re.astype(F32), lam_im.astype(F32)
    step = jnp.exp(log_step.astype(F32))[..., None]
    mag = jnp.exp(lam_re * step)
    lb_re, lb_im = mag * jnp.cos(lam_im * step), mag * jnp.sin(lam_im * step)
    den = lam_re * lam_re + lam_im * lam_im
    q_re = ((lb_re - 1.0) * lam_re + lb_im * lam_im) / den
    q_im = (lb_im * lam_re - (lb_re - 1.0) * lam_im) / den
    lanes = lambda a: a.transpose(1, 0, 2).reshape(ng, 2 * npar)
    rows = jnp.stack([lanes(a) for a in (lb_re, lb_im, q_re, q_im)], axis=1)
    rows = jnp.pad(rows, ((0, 0), (0, 4), (0, 0)))
    ci = jnp.arange(nc)[:, None, None]
    co = 2 * jnp.arange(half)[None, :, None] + jnp.arange(2)[None, None, :]
    dsel = (ci == co).astype(F32).reshape(nc * half, 2)
    dcol = jnp.repeat(d.astype(F32).reshape(ng, nc), half, axis=1)[..., None] * dsel[None]
    cols = jnp.concatenate([rows[:, :2].transpose(0, 2, 1), dcol], axis=2)
    perm = jnp.concatenate([jnp.arange(0, nc, 2), jnp.arange(1, nc, 2)])
    bc = jnp.stack([b_re.astype(F32).transpose(1, 3, 0, 2).reshape(ng, nc, 2 * npar),
                    b_im.astype(F32).transpose(1, 3, 0, 2).reshape(ng, nc, 2 * npar),
                    c_re.astype(F32).transpose(1, 2, 0, 3).reshape(ng, nc, 2 * npar)[:, perm],
                    c_im.astype(F32).transpose(1, 2, 0, 3).reshape(ng, nc, 2 * npar)[:, perm]],
                   axis=1)
    return rows, cols, bc


def _merge_kernel(yt_ref, ga_ref, pb_ref, x_ref, wglut_ref, wpa_ref, wout_ref, h_ref):
    ts = TOKEN_SUB_WIDE

    def body(r0, sub):
        rows = pl.ds(r0, ts)
        ncs = ts // CHUNK
        y = jax.nn.gelu(jnp.concatenate([yt_ref[:, sub * ncs + c, :] for c in range(ncs)], axis=1))
        gl = _dot(wglut_ref[...], y.astype(BF16))
        ya = (y * jax.nn.sigmoid(gl)).T.astype(BF16)
        merged = ga_ref[rows, :].astype(F32) * _dot(ya, wpa_ref[...]) + pb_ref[rows, :].astype(F32)
        h_ref[rows, :] = x_ref[rows, :] + _dot(merged.astype(BF16), wout_ref[...])

    _sub_loop(x_ref.shape[0], body, ts)


def _merge(yt3, ga, pb, x2, wglut, wpa, wout):
    t, d = x2.shape
    sw = yt3.shape[0]
    tm = TOKEN_BLOCK
    return pl.pallas_call(
        _merge_kernel,
        grid=(t // tm,),
        in_specs=[pl.BlockSpec((sw, tm // CHUNK, CHUNK), lambda i: (0, i, 0)),
                  pl.BlockSpec((tm, d), lambda i: (i, 0)),
                  pl.BlockSpec((tm, d), lambda i: (i, 0)),
                  pl.BlockSpec((tm, d), lambda i: (i, 0)),
                  _const_spec(wglut.shape), _const_spec(wpa.shape), _const_spec(wout.shape)],
        out_specs=pl.BlockSpec((tm, d), lambda i: (i, 0)),
        out_shape=jax.ShapeDtypeStruct((t, d), F32),
        compiler_params=_cparams("arbitrary"),
        name="merge",
    )(yt3, ga, pb, x2, wglut, wpa, wout)


def _xattn_kernel(h_ref, g_ref, wq_ref, k_ref, v_ref, wo_ref, o_ref):
    ts = TOKEN_SUB_WIDE

    def body(r0, sub):
        rows = pl.ds(r0, ts)
        h = h_ref[0, rows, :]
        hn = _rms(h, g_ref[...]).astype(BF16)
        q = _dot(hn, wq_ref[...]).astype(BF16)
        hd = q.shape[1] // XATTN_HEADS
        outs = []
        for hh in range(XATTN_HEADS):
            sl = slice(hh * hd, (hh + 1) * hd)
            s = _dot_nt(q[:, sl], k_ref[0, :, sl])
            e = jnp.exp(s - jnp.max(s, axis=-1, keepdims=True))
            p = e * (1.0 / jnp.sum(e, axis=-1, keepdims=True))
            outs.append(_dot(p.astype(BF16), v_ref[0, :, sl]))
        o = jnp.concatenate(outs, axis=1).astype(BF16)
        o_ref[0, rows, :] = h + _dot(o, wo_ref[...])

    _sub_loop(h_ref.shape[1], body, ts)


def _xattn(h3, g, wq, k, v, wo):
    b, l, d = h3.shape
    m = k.shape[1]
    tm = TOKEN_BLOCK
    return pl.pallas_call(
        _xattn_kernel,
        grid=(b, l // tm),
        in_specs=[pl.BlockSpec((1, tm, d), lambda i, j: (i, j, 0)),
                  _const_spec(g.shape), _const_spec(wq.shape),
                  pl.BlockSpec((1, m, d), lambda i, j: (i, 0, 0)),
                  pl.BlockSpec((1, m, d), lambda i, j: (i, 0, 0)),
                  _const_spec(wo.shape)],
        out_specs=pl.BlockSpec((1, tm, d), lambda i, j: (i, j, 0)),
        out_shape=jax.ShapeDtypeStruct((b, l, d), F32),
        compiler_params=_cparams("arbitrary", "arbitrary"),
        name="xattn",
    )(h3, g, wq, k, v, wo)


def _ffn_kernel(h_ref, g_ref, wg_ref, wu_ref, wd_ref, gf_ref, o_ref):
    def body(r0, sub):
        rows = pl.ds(r0, TOKEN_SUB)
        h = h_ref[rows, :]
        hn = _rms(h, g_ref[...]).astype(BF16)
        act = (jax.nn.silu(_dot(hn, wg_ref[...])) * _dot(hn, wu_ref[...])).astype(BF16)
        o_ref[rows, :] = _rms(h + _dot(act, wd_ref[...]), gf_ref[...])

    _sub_loop(h_ref.shape[0], body)


def _ffn(h2, g, wg, wu, wd, gf):
    t, d = h2.shape
    tm = TOKEN_BLOCK
    consts = (g, wg, wu, wd, gf)
    return pl.pallas_call(
        _ffn_kernel,
        grid=(t // tm,),
        in_specs=[pl.BlockSpec((tm, d), lambda i: (i, 0))] + [_const_spec(c.shape) for c in consts],
        out_specs=pl.BlockSpec((tm, d), lambda i: (i, 0)),
        out_shape=jax.ShapeDtypeStruct((t, d), F32),
        compiler_params=_cparams("arbitrary"),
        name="ffn",
    )(h2, *consts)


def _layer(h, kmem, vmem, mix_g, w_in, s5p, w_glu, ln_g, ln_b, sgu_w, sgu_bias, wpa, wpb, wout,
           xg, wq, wxo, fg, wgate, wup, wdown, final_g):
    b, l, d = h.shape
    t = b * l
    sw = w_glu.shape[0]
    su = ln_g.shape[0]
    row = lambda a: a.astype(F32).reshape(1, -1)
    x2 = h.reshape(t, d)
    bias_full = jnp.repeat(sgu_bias.astype(F32).T, su // SGU_HEADS, axis=1)
    xat3, ga, pb = _in_proj(
        x2, row(mix_g), w_in[:, :sw].T.astype(BF16), w_in[:, sw:].astype(BF16),
        row(ln_g), row(ln_b), sgu_w.astype(BF16), bias_full, wpb.astype(BF16))
    yt3 = _s5(xat3, _s5_tables(*s5p), b)
    h1 = _merge(yt3, ga, pb, x2, w_glu.T.astype(BF16), wpa.astype(BF16), wout.astype(BF16))
    h2 = _xattn(h1.reshape(b, l, d), row(xg), wq.astype(BF16), kmem, vmem, wxo.astype(BF16))
    out = _ffn(h2.reshape(t, d), row(fg), wgate.astype(BF16), wup.astype(BF16), wdown.astype(BF16),
               row(final_g))
    return out.reshape(b, l, d)


def kernel(x, mem, mix_norm_g, w_in, s5_lam_re, s5_lam_im, s5_log_step, s5_b_re, s5_b_im, s5_c_re, s5_c_im, s5_d, s5_w_glu, sgu_ln_g, sgu_ln_b, sgu_w, sgu_bias, w_proj_a, w_proj_b, w_out, xattn_norm_g, mem_norm_g, w_q, w_k, w_v, w_xo, ffn_norm_g, w_gate, w_up, w_down, final_norm_g):
    depth = w_in.shape[0]
    assert depth == 1, "the final rms_norm is fused into the last layer's ffn kernel"
    hd = x.shape[-1] // XATTN_HEADS
    i = 0
    kmem, vmem = _mem_kv(mem, mem_norm_g.astype(F32).reshape(1, -1), w_k[i].astype(BF16),
                         w_v[i].astype(BF16), hd ** -0.5)
    s5p = (s5_lam_re[i], s5_lam_im[i], s5_log_step[i], s5_b_re[i], s5_b_im[i], s5_c_re[i],
           s5_c_im[i], s5_d[i])
    return _layer(x, kmem, vmem, mix_norm_g[i], w_in[i], s5p, s5_w_glu[i], sgu_ln_g[i], sgu_ln_b[i],
                  sgu_w[i], sgu_bias[i], w_proj_a[i], w_proj_b[i], w_out[i], xattn_norm_g[i], w_q[i],
                  w_xo[i], ffn_norm_g[i], w_gate[i], w_up[i], w_down[i], final_norm_g)
```

```python
import functools
from typing import NamedTuple

import jax
import jax.numpy as jnp
from jax import lax
from jax.experimental import pallas as pl
from jax.experimental.pallas import tpu as pltpu

F32 = jnp.float32
BF16 = jnp.bfloat16

EPS = 1e-6
LANES = 128
CHUNK = 128
S5_GROUP = 16
S5_STATE = 64
SGU_HEADS = 8
XATTN_HEADS = 4
VMEM_LIMIT = 56 * 1024 * 1024
TOKEN_BLOCK = 1024
TOKEN_SUB = 256
TOKEN_SUB_WIDE = 512


def _cparams(*sem):
    return pltpu.CompilerParams(dimension_semantics=sem, vmem_limit_bytes=VMEM_LIMIT)


def _const_spec(shape):
    nd = len(shape)
    return pl.BlockSpec(shape, lambda *_: (0,) * nd, pipeline_mode=pl.Buffered(1))


def _rms(x, g):
    return x * lax.rsqrt(jnp.mean(x * x, axis=-1, keepdims=True) + EPS) * g


def _dot(a, b):
    return jnp.dot(a, b, preferred_element_type=F32)


def _dot_nt(a, b):
    return lax.dot_general(a, b, (((1,), (1,)), ((), ())), preferred_element_type=F32)


def _sub_loop(n_tokens, body, ts=TOKEN_SUB):
    def step(sub, carry):
        body(pl.multiple_of(sub * ts, ts), sub)
        return carry
    lax.fori_loop(0, n_tokens // ts, step, 0, unroll=2)


def _mem_kv_kernel(mem_ref, g_ref, wk_ref, wv_ref, k_ref, v_ref, *, scale):
    mn = _rms(mem_ref[0], g_ref[...]).astype(BF16)
    k_ref[0] = (_dot(mn, wk_ref[...]) * scale).astype(BF16)
    v_ref[0] = _dot(mn, wv_ref[...]).astype(BF16)


def _mem_kv(mem, g, wk, wv, scale):
    b, m, d = mem.shape
    return pl.pallas_call(
        functools.partial(_mem_kv_kernel, scale=scale),
        grid=(b,),
        in_specs=[pl.BlockSpec((1, m, d), lambda i: (i, 0, 0)),
                  _const_spec((1, d)), _const_spec((d, d)), _const_spec((d, d))],
        out_specs=[pl.BlockSpec((1, m, d), lambda i: (i, 0, 0))] * 2,
        out_shape=[jax.ShapeDtypeStruct((b, m, d), BF16)] * 2,
        compiler_params=_cparams("arbitrary"),
        name="mem_kv",
    )(mem, g, wk, wv)


def _in_proj_kernel(x_ref, g_ref, wat_ref, wr_ref, lng_ref, lnb_ref,
                    ws_ref, bias_ref, wpb_ref, xat_ref, ga_ref, pb_ref):
    def body(r0, sub):
        rows = pl.ds(r0, TOKEN_SUB)
        n = _rms(x_ref[rows, :], g_ref[...]).astype(BF16)
        xat = _dot_nt(wat_ref[...], n)
        for c in range(TOKEN_SUB // CHUNK):
            xat_ref[:, sub * (TOKEN_SUB // CHUNK) + c, :] = xat[:, c * CHUNK:(c + 1) * CHUNK]
        su = lng_ref.shape[1]
        proj = _dot(n, wr_ref[...])
        ga_ref[rows, :] = jax.nn.sigmoid(proj[:, 2 * su:2 * su + ga_ref.shape[1]]).astype(BF16)
        zv = jax.nn.gelu(proj[:, su:2 * su])
        zc = zv - jnp.mean(zv, axis=-1, keepdims=True)
        zvn = (zc * lax.rsqrt(jnp.mean(zc * zc, axis=-1, keepdims=True) + EPS) * lng_ref[...]
               + lnb_ref[...]).astype(BF16)
        hd = zvn.shape[1] // SGU_HEADS
        ncs = TOKEN_SUB // CHUNK
        mixed = [_dot(ws_ref[h], jnp.concatenate(
            [zvn[c * CHUNK:(c + 1) * CHUNK, h * hd:(h + 1) * hd] for c in range(ncs)], axis=1))
            for h in range(SGU_HEADS)]
        sv = jnp.concatenate(
            [jnp.concatenate([m[:, c * hd:(c + 1) * hd] for m in mixed], axis=1) + bias_ref[...]
             for c in range(ncs)], axis=0)
        zu = jax.nn.gelu(proj[:, :su])
        yb = (zu * sv).astype(BF16)
        gb = jax.nn.sigmoid(proj[:, 2 * su + ga_ref.shape[1]:])
        pb_ref[rows, :] = (gb * _dot(yb, wpb_ref[...])).astype(BF16)

    _sub_loop(x_ref.shape[0], body)


def _in_proj(x2, g, wat, wrest, lng, lnb, ws, bias_full, wpb):
    t, d = x2.shape
    sw = wat.shape[0]
    tm = TOKEN_BLOCK
    consts = (g, wat, wrest, lng, lnb, ws, bias_full, wpb)
    return pl.pallas_call(
        _in_proj_kernel,
        grid=(t // tm,),
        in_specs=[pl.BlockSpec((tm, d), lambda i: (i, 0))] + [_const_spec(c.shape) for c in consts],
        out_specs=[pl.BlockSpec((sw, tm // CHUNK, CHUNK), lambda i: (0, i, 0)),
                   pl.BlockSpec((tm, d), lambda i: (i, 0)),
                   pl.BlockSpec((tm, d), lambda i: (i, 0))],
        out_shape=[jax.ShapeDtypeStruct((sw, t // CHUNK, CHUNK), F32),
                   jax.ShapeDtypeStruct((t, d), BF16),
                   jax.ShapeDtypeStruct((t, d), BF16)],
        compiler_params=_cparams("arbitrary"),
        name="in_proj",
    )(x2, *consts)


def _cmul(ar, ai, br, bi):
    return ar * br - ai * bi, ar * bi + ai * br


def _cpow(br, bi, e, nbits):
    rr = ri = None
    for k in range(nbits):
        bit = ((e >> k) & 1) == 1
        sr, si = jnp.where(bit, br, 1.0), jnp.where(bit, bi, 0.0)
        rr, ri = (sr, si) if rr is None else _cmul(rr, ri, sr, si)
        if k + 1 < nbits:
            br, bi = _cmul(br, bi, br, bi)
    return rr, ri


class _S5Tabs(NamedTuple):
    kf: tuple
    kb: tuple
    lt: tuple
    et: tuple
    bb: tuple
    cc: tuple
    a: tuple


def _s5_tables_in_kernel(row_ref, col_ref, bc_ref):
    gc, half, ns = S5_GROUP, S5_GROUP // 2, 2 * S5_STATE
    sq = (CHUNK, CHUNK)
    sub = lax.broadcasted_iota(jnp.int32, sq, 0)
    lan = lax.broadcasted_iota(jnp.int32, sq, 1)
    lb_re, lb_im = row_ref[0, 0:1, :], row_ref[0, 1:2, :]
    q_re, q_im = row_ref[0, 2:3, :], row_ref[0, 3:4, :]
    fwd_lane = lan < S5_STATE
    lt = _cpow(lb_re, lb_im, jnp.where(fwd_lane, CHUNK - 1 - sub, sub), 7)
    et = _cpow(lb_re, lb_im, jnp.where(fwd_lane, sub + 1, CHUNK - sub), 8)
    p_re, p_im = _cpow(col_ref[0, :, 0:1], col_ref[0, :, 1:2],
                       jnp.where(sub < S5_STATE, lan, (CHUNK - lan) & (CHUNK - 1)), 7)
    a_re, a_im = lb_re, lb_im
    for _ in range(7):
        a_re, a_im = _cmul(a_re, a_im, a_re, a_im)
    bb = _cmul(q_re, q_im, bc_ref[0, 0], bc_ref[0, 1])
    c_re, c_im = bc_ref[0, 2], bc_ref[0, 3]

    rhs_f = jnp.concatenate([p_re[:S5_STATE], p_im[:S5_STATE]], axis=0)
    rhs_b = jnp.concatenate([p_re[S5_STATE:], p_im[S5_STATE:]], axis=0)
    lane_f = lax.broadcasted_iota(jnp.int32, (gc * half, ns), 1) < S5_STATE
    lane_tap = lax.broadcasted_iota(jnp.int32, (gc * half, CHUNK), 1)
    taps = []
    for par in range(2):
        cpr, cpi = c_re[par * half:(par + 1) * half], c_im[par * half:(par + 1) * half]
        cb = [_cmul(cpr, cpi, bb[0][ci:ci + 1], bb[1][ci:ci + 1]) for ci in range(gc)]
        cb_re = jnp.concatenate([v[0] for v in cb], axis=0)
        cb_im = jnp.concatenate([v[1] for v in cb], axis=0)
        lhs_f = jnp.where(lane_f, cb_re, -pltpu.roll(cb_im, S5_STATE, 1))
        lhs_b = jnp.where(lane_f, pltpu.roll(cb_re, S5_STATE, 1), -cb_im)
        kf = jnp.dot(lhs_f, rhs_f, preferred_element_type=F32, precision=lax.Precision.HIGHEST)
        kb = jnp.dot(lhs_b, rhs_b, preferred_element_type=F32, precision=lax.Precision.HIGHEST)
        kf = kf + jnp.where(lane_tap == 0, kb[:, 0:1] + col_ref[0, :, 2 + par:3 + par], 0.0)
        taps.append((kf, kb))

    return _S5Tabs((taps[0][0], taps[1][0]), (taps[0][1], taps[1][1]),
                   lt, et, bb, (c_re, c_im), (a_re, a_im))


def _s5_build_small(tabs, slot, bm_ref, cm_ref, a_ref):
    half = S5_GROUP // 2
    for ch in range(S5_GROUP):
        w_re, w_im = _cmul(*tabs.lt, tabs.bb[0][ch:ch + 1], tabs.bb[1][ch:ch + 1])
        bm_ref[slot, ch * CHUNK:(ch + 1) * CHUNK, :] = jnp.concatenate([w_re, w_im], axis=1).astype(BF16)
        r = (ch % 2) * half + ch // 2
        g_re, g_im = _cmul(*tabs.et, tabs.cc[0][r:r + 1], tabs.cc[1][r:r + 1])
        cm_ref[slot, ch * CHUNK:(ch + 1) * CHUNK, :] = jnp.concatenate([g_re, -g_im], axis=1).astype(BF16)
    a_ref[slot, 0] = jnp.broadcast_to(tabs.a[0], a_ref.shape[2:])
    a_ref[slot, 1] = jnp.broadcast_to(tabs.a[1], a_ref.shape[2:])


def _s5_build_pair(tabs, slot, q, m_ref):
    half = S5_GROUP // 2
    sq = (CHUNK, CHUNK)
    upper = lax.broadcasted_iota(jnp.int32, sq, 1) >= lax.broadcasted_iota(jnp.int32, sq, 0)
    for ci in range(S5_GROUP):
        r = ci * half + q
        blocks = []
        for par in range(2):
            vf = jnp.broadcast_to(tabs.kf[par][r:r + 1], sq)
            vb = jnp.broadcast_to(tabs.kb[par][r:r + 1], sq)
            cf = pltpu.roll(vf, 0, 1, stride=1, stride_axis=0)
            cbw = pltpu.roll(vb, 0, 1, stride=1, stride_axis=0)
            blocks.append(jnp.where(upper, cf, cbw))
        m_ref[slot, ci * CHUNK:(ci + 1) * CHUNK, 2 * q * CHUNK:(2 * q + 2) * CHUNK] = (
            jnp.concatenate(blocks, axis=1).astype(BF16))


def _s5_apply(x_ref, y_ref, gi, slot, m_ref, bm_ref, cm_ref, a_ref, s_ref, xf_ref, xb_ref, nbatch,
              between):
    gc, ns = S5_GROUP, 2 * S5_STATE
    nrows = x_ref.shape[1]
    nch = nrows // nbatch
    x = jnp.concatenate([x_ref[gi * gc + ci].astype(BF16) for ci in range(gc)], axis=1)
    s = _dot(x, bm_ref[slot])
    s_ref[0] = s[:, :ns]
    s_ref[1] = s[:, ns:]
    a_re, a_im = a_ref[slot, 0, 0:1, :], a_ref[slot, 1, 0:1, :]
    is_fwd = lax.broadcasted_iota(jnp.int32, (nbatch, ns), 1) < S5_STATE
    st_re = st_im = jnp.zeros((nbatch, ns), F32)
    for i in range(nch):
        rows_f = pl.ds(i, nbatch, stride=nch)
        rows_b = pl.ds(nch - 1 - i, nbatch, stride=nch)
        xf_ref[0, rows_f, :] = st_re
        xf_ref[1, rows_f, :] = st_im
        xb_ref[0, rows_b, :] = st_re
        xb_ref[1, rows_b, :] = st_im
        if i + 1 < nch:
            s_re = jnp.where(is_fwd, s_ref[0, rows_f, :], s_ref[0, rows_b, :])
            s_im = jnp.where(is_fwd, s_ref[1, rows_f, :], s_ref[1, rows_b, :])
            n_re, n_im = _cmul(a_re, a_im, st_re, st_im)
            st_re, st_im = n_re + s_re, n_im + s_im
    is_fwd_all = lax.broadcasted_iota(jnp.int32, (nrows, ns), 1) < S5_STATE
    xin = jnp.concatenate([jnp.where(is_fwd_all, xf_ref[0], xb_ref[0]),
                           jnp.where(is_fwd_all, xf_ref[1], xb_ref[1])], axis=1).astype(BF16)
    for q in range(gc // 2):
        between(q)
        cols = slice(2 * q * CHUNK, (2 * q + 2) * CHUNK)
        y = _dot(x, m_ref[slot, :, cols]) + _dot_nt(xin, cm_ref[slot, cols, :])
        y_ref[gi * gc + 2 * q] = y[:, :CHUNK]
        y_ref[gi * gc + 2 * q + 1] = y[:, CHUNK:]


def _s5_kernel(x_ref, row0, col0, bc0, row1, col1, bc1,
               y_ref, m_ref, bm_ref, cm_ref, a_ref, s_ref, xf_ref, xb_ref, *, nbatch):
    scr = (m_ref, bm_ref, cm_ref, a_ref, s_ref, xf_ref, xb_ref)
    k = pl.program_id(0)

    @pl.when(k == 0)
    def _():
        tabs = _s5_tables_in_kernel(row0, col0, bc0)
        _s5_build_small(tabs, 0, bm_ref, cm_ref, a_ref)
        for q in range(S5_GROUP // 2):
            _s5_build_pair(tabs, 0, q, m_ref)

    for cur in range(2):
        @pl.when(k % 2 == cur)
        def _():
            tabs = _s5_tables_in_kernel(row1, col1, bc1)
            _s5_build_small(tabs, 1 - cur, bm_ref, cm_ref, a_ref)
            _s5_apply(x_ref, y_ref, 0, cur, *scr, nbatch,
                      functools.partial(_s5_build_pair, tabs, 1 - cur, m_ref=m_ref))


def _s5(xat3, tabs, nbatch):
    sw, nrows, _ = xat3.shape
    gc, ns = S5_GROUP, 2 * S5_STATE
    nsteps = sw // gc
    last = nsteps - 1

    def tspecs(index):
        return [pl.BlockSpec((1,) + a.shape[1:], lambda k, n=a.ndim: (index(k),) + (0,) * (n - 1))
                for a in tabs]

    return pl.pallas_call(
        functools.partial(_s5_kernel, nbatch=nbatch),
        grid=(nsteps,),
        in_specs=[pl.BlockSpec((gc, nrows, CHUNK), lambda k: (k, 0, 0))]
                 + tspecs(lambda k: 0) + tspecs(lambda k: jnp.minimum(k + 1, last)),
        out_specs=pl.BlockSpec((gc, nrows, CHUNK), lambda k: (k, 0, 0)),
        out_shape=jax.ShapeDtypeStruct((sw, nrows, CHUNK), F32),
        scratch_shapes=[pltpu.VMEM((2, gc * CHUNK, gc * CHUNK), BF16),
                        pltpu.VMEM((2, gc * CHUNK, 2 * ns), BF16),
                        pltpu.VMEM((2, gc * CHUNK, 2 * ns), BF16),
                        pltpu.VMEM((2, 2, 8, ns), F32),
                        pltpu.VMEM((2, nrows, ns), F32),
                        pltpu.VMEM((2, nrows, ns), F32),
                        pltpu.VMEM((2, nrows, ns), F32)],
        compiler_params=_cparams("arbitrary"),
        name="s5",
    )(xat3, *tabs, *tabs)


def _s5_tables(lam_re, lam_im, log_step, b_re, b_im, c_re, c_im, d):
    ng, npar, nc = b_re.shape[1], b_re.shape[2], b_re.shape[3]
    half = nc // 2
    lam_re, lam_im = lam_re.astype(F32), lam_im.astype(F32)
    step = jnp.exp(log_step.astype(F32))[..., None]
    mag = jnp.exp(lam_re * step)
    lb_re, lb_im = mag * jnp.cos(lam_im * step), mag * jnp.sin(lam_im * step)
    den = lam_re * lam_re + lam_im * lam_im
    q_re = ((lb_re - 1.0) * lam_re + lb_im * lam_im) / den
    q_im = (lb_im * lam_re - (lb_re - 1.0) * lam_im) / den
    lanes = lambda a: a.transpose(1, 0, 2).reshape(ng, 2 * npar)
    rows = jnp.stack([lanes(a) for a in (lb_re, lb_im, q_re, q_im)], axis=1)
    rows = jnp.pad(rows, ((0, 0), (0, 4), (0, 0)))
    ci = jnp.arange(nc)[:, None, None]
    co = 2 * jnp.arange(half)[None, :, None] + jnp.arange(2)[None, None, :]
    dsel = (ci == co).astype(F32).reshape(nc * half, 2)
    dcol = jnp.repeat(d.astype(F32).reshape(ng, nc), half, axis=1)[..., None] * dsel[None]
    cols = jnp.concatenate([rows[:, :2].transpose(0, 2, 1), dcol], axis=2)
    perm = jnp.concatenate([jnp.arange(0, nc, 2), jnp.arange(1, nc, 2)])
    bc = jnp.stack([b_re.astype(F32).transpose(1, 3, 0, 2).reshape(ng, nc, 2 * npar),
                    b_im.astype(F32).transpose(1, 3, 0, 2).reshape(ng, nc, 2 * npar),
                    c_re.astype(F32).transpose(1, 2, 0, 3).reshape(ng, nc, 2 * npar)[:, perm],
                    c_im.astype(F32).transpose(1, 2, 0, 3).reshape(ng, nc, 2 * npar)[:, perm]],
                   axis=1)
    return rows, cols, bc


def _merge_xattn_kernel(yt_ref, ga_ref, pb_ref, x_ref, wglut_ref, wpa_ref, wout_ref,
                        g_ref, wq_ref, k_ref, v_ref, wo_ref, o_ref):
    ts = TOKEN_SUB_WIDE

    def body(r0, sub):
        rows = pl.ds(r0, ts)
        ncs = ts // CHUNK
        y = jax.nn.gelu(jnp.concatenate([yt_ref[:, sub * ncs + c, :] for c in range(ncs)], axis=1))
        gl = _dot(wglut_ref[...], y.astype(BF16))
        ya = (y * jax.nn.sigmoid(gl)).T.astype(BF16)
        merged = ga_ref[rows, :].astype(F32) * _dot(ya, wpa_ref[...]) + pb_ref[rows, :].astype(F32)
        h = x_ref[rows, :] + _dot(merged.astype(BF16), wout_ref[...])

        hn = _rms(h, g_ref[...]).astype(BF16)
        q = _dot(hn, wq_ref[...]).astype(BF16)
        hd = q.shape[1] // XATTN_HEADS
        outs = []
        for hh in range(XATTN_HEADS):
            sl = slice(hh * hd, (hh + 1) * hd)
            s = _dot_nt(q[:, sl], k_ref[0, :, sl])
            e = jnp.exp(s - jnp.max(s, axis=-1, keepdims=True))
            p = e * (1.0 / jnp.sum(e, axis=-1, keepdims=True))
            outs.append(_dot(p.astype(BF16), v_ref[0, :, sl]))
        o = jnp.concatenate(outs, axis=1).astype(BF16)
        o_ref[rows, :] = h + _dot(o, wo_ref[...])

    _sub_loop(x_ref.shape[0], body, ts)


def _merge_xattn(yt3, ga, pb, x2, wglut, wpa, wout, g, wq, k, v, wo):
    t, d = x2.shape
    sw = yt3.shape[0]
    nb, m = k.shape[0], k.shape[1]
    tm = TOKEN_BLOCK
    per_batch = t // nb // tm
    tok = lambda i: (i, 0)
    mem = lambda i: (i // per_batch, 0, 0)
    consts = (wglut, wpa, wout, g, wq)
    return pl.pallas_call(
        _merge_xattn_kernel,
        grid=(t // tm,),
        in_specs=[pl.BlockSpec((sw, tm // CHUNK, CHUNK), lambda i: (0, i, 0)),
                  pl.BlockSpec((tm, d), tok), pl.BlockSpec((tm, d), tok), pl.BlockSpec((tm, d), tok)]
                 + [_const_spec(c.shape) for c in consts]
                 + [pl.BlockSpec((1, m, d), mem), pl.BlockSpec((1, m, d), mem), _const_spec(wo.shape)],
        out_specs=pl.BlockSpec((tm, d), tok),
        out_shape=jax.ShapeDtypeStruct((t, d), F32),
        compiler_params=_cparams("arbitrary"),
        name="merge_xattn",
    )(yt3, ga, pb, x2, *consts, k, v, wo)


def _ffn_kernel(h_ref, g_ref, wg_ref, wu_ref, wd_ref, gf_ref, o_ref):
    def body(r0, sub):
        rows = pl.ds(r0, TOKEN_SUB)
        h = h_ref[rows, :]
        hn = _rms(h, g_ref[...]).astype(BF16)
        act = (jax.nn.silu(_dot(hn, wg_ref[...])) * _dot(hn, wu_ref[...])).astype(BF16)
        o_ref[rows, :] = _rms(h + _dot(act, wd_ref[...]), gf_ref[...])

    _sub_loop(h_ref.shape[0], body)


def _ffn(h2, g, wg, wu, wd, gf):
    t, d = h2.shape
    tm = TOKEN_BLOCK
    consts = (g, wg, wu, wd, gf)
    return pl.pallas_call(
        _ffn_kernel,
        grid=(t // tm,),
        in_specs=[pl.BlockSpec((tm, d), lambda i: (i, 0))] + [_const_spec(c.shape) for c in consts],
        out_specs=pl.BlockSpec((tm, d), lambda i: (i, 0)),
        out_shape=jax.ShapeDtypeStruct((t, d), F32),
        compiler_params=_cparams("arbitrary"),
        name="ffn",
    )(h2, *consts)


def _layer(h, kmem, vmem, mix_g, w_in, s5p, w_glu, ln_g, ln_b, sgu_w, sgu_bias, wpa, wpb, wout,
           xg, wq, wxo, fg, wgate, wup, wdown, final_g):
    b, l, d = h.shape
    t = b * l
    sw = w_glu.shape[0]
    su = ln_g.shape[0]
    row = lambda a: a.astype(F32).reshape(1, -1)
    x2 = h.reshape(t, d)
    bias_full = jnp.repeat(sgu_bias.astype(F32).T, su // SGU_HEADS, axis=1)
    xat3, ga, pb = _in_proj(
        x2, row(mix_g), w_in[:, :sw].T.astype(BF16), w_in[:, sw:].astype(BF16),
        row(ln_g), row(ln_b), sgu_w.astype(BF16), bias_full, wpb.astype(BF16))
    yt3 = _s5(xat3, _s5_tables(*s5p), b)
    h2 = _merge_xattn(yt3, ga, pb, x2, w_glu.T.astype(BF16), wpa.astype(BF16), wout.astype(BF16),
                      row(xg), wq.astype(BF16), kmem, vmem, wxo.astype(BF16))
    out = _ffn(h2, row(fg), wgate.astype(BF16), wup.astype(BF16), wdown.astype(BF16),
               row(final_g))
    return out.reshape(b, l, d)


def kernel(x, mem, mix_norm_g, w_in, s5_lam_re, s5_lam_im, s5_log_step, s5_b_re, s5_b_im, s5_c_re, s5_c_im, s5_d, s5_w_glu, sgu_ln_g, sgu_ln_b, sgu_w, sgu_bias, w_proj_a, w_proj_b, w_out, xattn_norm_g, mem_norm_g, w_q, w_k, w_v, w_xo, ffn_norm_g, w_gate, w_up, w_down, final_norm_g):
    depth = w_in.shape[0]
    assert depth == 1, "the final rms_norm is fused into the last layer's ffn kernel"
    hd = x.shape[-1] // XATTN_HEADS
    i = 0
    kmem, vmem = _mem_kv(mem, mem_norm_g.astype(F32).reshape(1, -1), w_k[i].astype(BF16),
                         w_v[i].astype(BF16), hd ** -0.5)
    s5p = (s5_lam_re[i], s5_lam_im[i], s5_log_step[i], s5_b_re[i], s5_b_im[i], s5_c_re[i],
           s5_c_im[i], s5_d[i])
    return _layer(x, kmem, vmem, mix_norm_g[i], w_in[i], s5p, s5_w_glu[i], sgu_ln_g[i], sgu_ln_b[i],
                  sgu_w[i], sgu_bias[i], w_proj_a[i], w_proj_b[i], w_out[i], xattn_norm_g[i], w_q[i],
                  w_xo[i], ffn_norm_g[i], w_gate[i], w_up[i], w_down[i], final_norm_g)
```

```python
import functools
from typing import NamedTuple

import jax
import jax.numpy as jnp
from jax import lax
from jax.experimental import pallas as pl
from jax.experimental.pallas import tpu as pltpu

F32 = jnp.float32
BF16 = jnp.bfloat16

EPS = 1e-6
LANES = 128
CHUNK = 128
S5_GROUP = 16
S5_STATE = 64
SGU_HEADS = 8
XATTN_HEADS = 4
VMEM_LIMIT = 56 * 1024 * 1024
TOKEN_BLOCK = 1024
TOKEN_SUB = 256
TOKEN_SUB_WIDE = 512


def _cparams(*sem):
    return pltpu.CompilerParams(dimension_semantics=sem, vmem_limit_bytes=VMEM_LIMIT)


def _const_spec(shape):
    nd = len(shape)
    return pl.BlockSpec(shape, lambda *_: (0,) * nd, pipeline_mode=pl.Buffered(1))


def _rms(x, g):
    return x * lax.rsqrt(jnp.mean(x * x, axis=-1, keepdims=True) + EPS) * g


def _dot(a, b):
    return jnp.dot(a, b, preferred_element_type=F32)


def _dot_nt(a, b):
    return lax.dot_general(a, b, (((1,), (1,)), ((), ())), preferred_element_type=F32)


def _sub_loop(n_tokens, body, ts=TOKEN_SUB):
    def step(sub, carry):
        body(pl.multiple_of(sub * ts, ts), sub)
        return carry
    lax.fori_loop(0, n_tokens // ts, step, 0, unroll=2)


def _mem_kv_kernel(mem_ref, g_ref, wk_ref, wv_ref, k_ref, v_ref, *, scale):
    mn = _rms(mem_ref[0], g_ref[...]).astype(BF16)
    k_ref[0] = (_dot(mn, wk_ref[...]) * scale).astype(BF16)
    v_ref[0] = _dot(mn, wv_ref[...]).astype(BF16)


def _mem_kv(mem, g, wk, wv, scale):
    b, m, d = mem.shape
    return pl.pallas_call(
        functools.partial(_mem_kv_kernel, scale=scale),
        grid=(b,),
        in_specs=[pl.BlockSpec((1, m, d), lambda i: (i, 0, 0)),
                  _const_spec((1, d)), _const_spec((d, d)), _const_spec((d, d))],
        out_specs=[pl.BlockSpec((1, m, d), lambda i: (i, 0, 0))] * 2,
        out_shape=[jax.ShapeDtypeStruct((b, m, d), BF16)] * 2,
        compiler_params=_cparams("arbitrary"),
        name="mem_kv",
    )(mem, g, wk, wv)


def _in_proj_kernel(x_ref, g_ref, wat_ref, win_ref, lng_ref, lnb_ref,
                    ws_ref, bias_ref, wpb_ref, xat_ref, ga_ref, pb_ref):
    def body(r0, sub):
        rows = pl.ds(r0, TOKEN_SUB)
        n = _rms(x_ref[rows, :], g_ref[...]).astype(BF16)
        xat = _dot_nt(wat_ref[...], n)
        for c in range(TOKEN_SUB // CHUNK):
            xat_ref[:, sub * (TOKEN_SUB // CHUNK) + c, :] = xat[:, c * CHUNK:(c + 1) * CHUNK]
        su = lng_ref.shape[1]
        proj = _dot(n, win_ref[:, wat_ref.shape[0]:])
        ga_ref[rows, :] = jax.nn.sigmoid(proj[:, 2 * su:2 * su + ga_ref.shape[1]]).astype(BF16)
        zv = jax.nn.gelu(proj[:, su:2 * su])
        zc = zv - jnp.mean(zv, axis=-1, keepdims=True)
        zvn = (zc * lax.rsqrt(jnp.mean(zc * zc, axis=-1, keepdims=True) + EPS) * lng_ref[...]
               + lnb_ref[...]).astype(BF16)
        hd = zvn.shape[1] // SGU_HEADS
        ncs = TOKEN_SUB // CHUNK
        mixed = [_dot(ws_ref[h], jnp.concatenate(
            [zvn[c * CHUNK:(c + 1) * CHUNK, h * hd:(h + 1) * hd] for c in range(ncs)], axis=1))
            for h in range(SGU_HEADS)]
        sv = jnp.concatenate(
            [jnp.concatenate([m[:, c * hd:(c + 1) * hd] for m in mixed], axis=1) + bias_ref[...]
             for c in range(ncs)], axis=0)
        zu = jax.nn.gelu(proj[:, :su])
        yb = (zu * sv).astype(BF16)
        gb = jax.nn.sigmoid(proj[:, 2 * su + ga_ref.shape[1]:])
        pb_ref[rows, :] = (gb * _dot(yb, wpb_ref[...])).astype(BF16)

    _sub_loop(x_ref.shape[0], body)


def _in_proj(x2, g, wat, win, lng, lnb, ws, bias_full, wpb):
    t, d = x2.shape
    sw = wat.shape[0]
    tm = TOKEN_BLOCK
    consts = (g, wat, win, lng, lnb, ws, bias_full, wpb)
    return pl.pallas_call(
        _in_proj_kernel,
        grid=(t // tm,),
        in_specs=[pl.BlockSpec((tm, d), lambda i: (i, 0))] + [_const_spec(c.shape) for c in consts],
        out_specs=[pl.BlockSpec((sw, tm // CHUNK, CHUNK), lambda i: (0, i, 0)),
                   pl.BlockSpec((tm, d), lambda i: (i, 0)),
                   pl.BlockSpec((tm, d), lambda i: (i, 0))],
        out_shape=[jax.ShapeDtypeStruct((sw, t // CHUNK, CHUNK), F32),
                   jax.ShapeDtypeStruct((t, d), BF16),
                   jax.ShapeDtypeStruct((t, d), BF16)],
        compiler_params=_cparams("arbitrary"),
        name="in_proj",
    )(x2, *consts)


def _cmul(ar, ai, br, bi):
    return ar * br - ai * bi, ar * bi + ai * br


def _cpow(br, bi, e, nbits):
    rr = ri = None
    for k in range(nbits):
        bit = ((e >> k) & 1) == 1
        sr, si = jnp.where(bit, br, 1.0), jnp.where(bit, bi, 0.0)
        rr, ri = (sr, si) if rr is None else _cmul(rr, ri, sr, si)
        if k + 1 < nbits:
            br, bi = _cmul(br, bi, br, bi)
    return rr, ri


class _S5Tabs(NamedTuple):
    kf: tuple
    kb: tuple
    lt: tuple
    et: tuple
    bb: tuple
    cc: tuple
    a: tuple


def _s5_tables_in_kernel(row_ref, col_ref, bc_ref):
    gc, half, ns = S5_GROUP, S5_GROUP // 2, 2 * S5_STATE
    sq = (CHUNK, CHUNK)
    sub = lax.broadcasted_iota(jnp.int32, sq, 0)
    lan = lax.broadcasted_iota(jnp.int32, sq, 1)
    lb_re, lb_im = row_ref[0, 0:1, :], row_ref[0, 1:2, :]
    q_re, q_im = row_ref[0, 2:3, :], row_ref[0, 3:4, :]
    fwd_lane = lan < S5_STATE
    lt = _cpow(lb_re, lb_im, jnp.where(fwd_lane, CHUNK - 1 - sub, sub), 7)
    et = _cpow(lb_re, lb_im, jnp.where(fwd_lane, sub + 1, CHUNK - sub), 8)
    p_re, p_im = _cpow(col_ref[0, :, 0:1], col_ref[0, :, 1:2],
                       jnp.where(sub < S5_STATE, lan, (CHUNK - lan) & (CHUNK - 1)), 7)
    a_re, a_im = lb_re, lb_im
    for _ in range(7):
        a_re, a_im = _cmul(a_re, a_im, a_re, a_im)
    bb = _cmul(q_re, q_im, bc_ref[0, 0], bc_ref[0, 1])
    c_re, c_im = bc_ref[0, 2], bc_ref[0, 3]

    rhs_f = jnp.concatenate([p_re[:S5_STATE], p_im[:S5_STATE]], axis=0)
    rhs_b = jnp.concatenate([p_re[S5_STATE:], p_im[S5_STATE:]], axis=0)
    lane_f = lax.broadcasted_iota(jnp.int32, (gc * half, ns), 1) < S5_STATE
    lane_tap = lax.broadcasted_iota(jnp.int32, (gc * half, CHUNK), 1)
    taps = []
    for par in range(2):
        cpr, cpi = c_re[par * half:(par + 1) * half], c_im[par * half:(par + 1) * half]
        cb = [_cmul(cpr, cpi, bb[0][ci:ci + 1], bb[1][ci:ci + 1]) for ci in range(gc)]
        cb_re = jnp.concatenate([v[0] for v in cb], axis=0)
        cb_im = jnp.concatenate([v[1] for v in cb], axis=0)
        lhs_f = jnp.where(lane_f, cb_re, -pltpu.roll(cb_im, S5_STATE, 1))
        lhs_b = jnp.where(lane_f, pltpu.roll(cb_re, S5_STATE, 1), -cb_im)
        kf = jnp.dot(lhs_f, rhs_f, preferred_element_type=F32, precision=lax.Precision.HIGHEST)
        kb = jnp.dot(lhs_b, rhs_b, preferred_element_type=F32, precision=lax.Precision.HIGHEST)
        kf = kf + jnp.where(lane_tap == 0, kb[:, 0:1] + col_ref[0, :, 2 + par:3 + par], 0.0)
        taps.append((kf, kb))

    return _S5Tabs((taps[0][0], taps[1][0]), (taps[0][1], taps[1][1]),
                   lt, et, bb, (c_re, c_im), (a_re, a_im))


def _s5_build_small(tabs, slot, bm_ref, cm_ref, a_ref):
    half = S5_GROUP // 2
    for ch in range(S5_GROUP):
        w_re, w_im = _cmul(*tabs.lt, tabs.bb[0][ch:ch + 1], tabs.bb[1][ch:ch + 1])
        bm_ref[slot, ch * CHUNK:(ch + 1) * CHUNK, :] = jnp.concatenate([w_re, w_im], axis=1).astype(BF16)
        r = (ch % 2) * half + ch // 2
        g_re, g_im = _cmul(*tabs.et, tabs.cc[0][r:r + 1], tabs.cc[1][r:r + 1])
        cm_ref[slot, ch * CHUNK:(ch + 1) * CHUNK, :] = jnp.concatenate([g_re, -g_im], axis=1).astype(BF16)
    a_ref[slot, 0] = jnp.broadcast_to(tabs.a[0], a_ref.shape[2:])
    a_ref[slot, 1] = jnp.broadcast_to(tabs.a[1], a_ref.shape[2:])


def _s5_build_pair(tabs, slot, q, m_ref):
    half = S5_GROUP // 2
    sq = (CHUNK, CHUNK)
    upper = lax.broadcasted_iota(jnp.int32, sq, 1) >= lax.broadcasted_iota(jnp.int32, sq, 0)
    for ci in range(S5_GROUP):
        r = ci * half + q
        blocks = []
        for par in range(2):
            vf = jnp.broadcast_to(tabs.kf[par][r:r + 1], sq)
            vb = jnp.broadcast_to(tabs.kb[par][r:r + 1], sq)
            cf = pltpu.roll(vf, 0, 1, stride=1, stride_axis=0)
            cbw = pltpu.roll(vb, 0, 1, stride=1, stride_axis=0)
            blocks.append(jnp.where(upper, cf, cbw))
        m_ref[slot, ci * CHUNK:(ci + 1) * CHUNK, 2 * q * CHUNK:(2 * q + 2) * CHUNK] = (
            jnp.concatenate(blocks, axis=1).astype(BF16))


def _s5_apply(x_ref, y_ref, gi, slot, m_ref, bm_ref, cm_ref, a_ref, s_ref, xf_ref, xb_ref, nbatch,
              between):
    gc, ns = S5_GROUP, 2 * S5_STATE
    nrows = x_ref.shape[1]
    nch = nrows // nbatch
    x = jnp.concatenate([x_ref[gi * gc + ci].astype(BF16) for ci in range(gc)], axis=1)
    s = _dot(x, bm_ref[slot])
    s_ref[0] = s[:, :ns]
    s_ref[1] = s[:, ns:]
    a_re, a_im = a_ref[slot, 0, 0:1, :], a_ref[slot, 1, 0:1, :]
    is_fwd = lax.broadcasted_iota(jnp.int32, (nbatch, ns), 1) < S5_STATE
    st_re = st_im = jnp.zeros((nbatch, ns), F32)
    for i in range(nch):
        rows_f = pl.ds(i, nbatch, stride=nch)
        rows_b = pl.ds(nch - 1 - i, nbatch, stride=nch)
        xf_ref[0, rows_f, :] = st_re
        xf_ref[1, rows_f, :] = st_im
        xb_ref[0, rows_b, :] = st_re
        xb_ref[1, rows_b, :] = st_im
        if i + 1 < nch:
            s_re = jnp.where(is_fwd, s_ref[0, rows_f, :], s_ref[0, rows_b, :])
            s_im = jnp.where(is_fwd, s_ref[1, rows_f, :], s_ref[1, rows_b, :])
            n_re, n_im = _cmul(a_re, a_im, st_re, st_im)
            st_re, st_im = n_re + s_re, n_im + s_im
    is_fwd_all = lax.broadcasted_iota(jnp.int32, (nrows, ns), 1) < S5_STATE
    xin = jnp.concatenate([jnp.where(is_fwd_all, xf_ref[0], xb_ref[0]),
                           jnp.where(is_fwd_all, xf_ref[1], xb_ref[1])], axis=1).astype(BF16)
    for q in range(gc // 2):
        between(q)
        cols = slice(2 * q * CHUNK, (2 * q + 2) * CHUNK)
        y = _dot(x, m_ref[slot, :, cols]) + _dot_nt(xin, cm_ref[slot, cols, :])
        y_ref[gi * gc + 2 * q] = y[:, :CHUNK]
        y_ref[gi * gc + 2 * q + 1] = y[:, CHUNK:]


def _s5_kernel(x_ref, row0, col0, bc0, row1, col1, bc1,
               y_ref, m_ref, bm_ref, cm_ref, a_ref, s_ref, xf_ref, xb_ref, *, nbatch):
    scr = (m_ref, bm_ref, cm_ref, a_ref, s_ref, xf_ref, xb_ref)
    k = pl.program_id(0)

    @pl.when(k == 0)
    def _():
        tabs = _s5_tables_in_kernel(row0, col0, bc0)
        _s5_build_small(tabs, 0, bm_ref, cm_ref, a_ref)
        for q in range(S5_GROUP // 2):
            _s5_build_pair(tabs, 0, q, m_ref)

    for cur in range(2):
        @pl.when(k % 2 == cur)
        def _():
            tabs = _s5_tables_in_kernel(row1, col1, bc1)
            _s5_build_small(tabs, 1 - cur, bm_ref, cm_ref, a_ref)
            _s5_apply(x_ref, y_ref, 0, cur, *scr, nbatch,
                      functools.partial(_s5_build_pair, tabs, 1 - cur, m_ref=m_ref))


def _s5(xat3, tabs, nbatch):
    sw, nrows, _ = xat3.shape
    gc, ns = S5_GROUP, 2 * S5_STATE
    nsteps = sw // gc
    last = nsteps - 1

    def tspecs(index):
        return [pl.BlockSpec((1,) + a.shape[1:], lambda k, n=a.ndim: (index(k),) + (0,) * (n - 1))
                for a in tabs]

    return pl.pallas_call(
        functools.partial(_s5_kernel, nbatch=nbatch),
        grid=(nsteps,),
        in_specs=[pl.BlockSpec((gc, nrows, CHUNK), lambda k: (k, 0, 0))]
                 + tspecs(lambda k: 0) + tspecs(lambda k: jnp.minimum(k + 1, last)),
        out_specs=pl.BlockSpec((gc, nrows, CHUNK), lambda k: (k, 0, 0)),
        out_shape=jax.ShapeDtypeStruct((sw, nrows, CHUNK), F32),
        scratch_shapes=[pltpu.VMEM((2, gc * CHUNK, gc * CHUNK), BF16),
                        pltpu.VMEM((2, gc * CHUNK, 2 * ns), BF16),
                        pltpu.VMEM((2, gc * CHUNK, 2 * ns), BF16),
                        pltpu.VMEM((2, 2, 8, ns), F32),
                        pltpu.VMEM((2, nrows, ns), F32),
                        pltpu.VMEM((2, nrows, ns), F32),
                        pltpu.VMEM((2, nrows, ns), F32)],
        compiler_params=_cparams("arbitrary"),
        name="s5",
    )(xat3, *tabs, *tabs)


def _s5_tables(lam_re, lam_im, log_step, b_re, b_im, c_re, c_im, d):
    ng, npar, nc = b_re.shape[1], b_re.shape[2], b_re.shape[3]
    half = nc // 2
    lam_re, lam_im = lam_re.astype(F32), lam_im.astype(F32)
    step = jnp.exp(log_step.astype(F32))[..., None]
    mag = jnp.exp(lam_re * step)
    lb_re, lb_im = mag * jnp.cos(lam_im * step), mag * jnp.sin(lam_im * step)
    den = lam_re * lam_re + lam_im * lam_im
    q_re = ((lb_re - 1.0) * lam_re + lb_im * lam_im) / den
    q_im = (lb_im * lam_re - (lb_re - 1.0) * lam_im) / den
    lanes = lambda a: a.transpose(1, 0, 2).reshape(ng, 2 * npar)
    rows = jnp.stack([lanes(a) for a in (lb_re, lb_im, q_re, q_im)], axis=1)
    rows = jnp.pad(rows, ((0, 0), (0, 4), (0, 0)))
    ci = jnp.arange(nc)[:, None, None]
    co = 2 * jnp.arange(half)[None, :, None] + jnp.arange(2)[None, None, :]
    dsel = (ci == co).astype(F32).reshape(nc * half, 2)
    dcol = jnp.repeat(d.astype(F32).reshape(ng, nc), half, axis=1)[..., None] * dsel[None]
    cols = jnp.concatenate([rows[:, :2].transpose(0, 2, 1), dcol], axis=2)
    perm = jnp.concatenate([jnp.arange(0, nc, 2), jnp.arange(1, nc, 2)])
    bc = jnp.stack([b_re.astype(F32).transpose(1, 3, 0, 2).reshape(ng, nc, 2 * npar),
                    b_im.astype(F32).transpose(1, 3, 0, 2).reshape(ng, nc, 2 * npar),
                    c_re.astype(F32).transpose(1, 2, 0, 3).reshape(ng, nc, 2 * npar)[:, perm],
                    c_im.astype(F32).transpose(1, 2, 0, 3).reshape(ng, nc, 2 * npar)[:, perm]],
                   axis=1)
    return rows, cols, bc


def _merge_xattn_kernel(yt_ref, ga_ref, pb_ref, x_ref, wglut_ref, wpa_ref, wout_ref,
                        g_ref, wq_ref, k_ref, v_ref, wo_ref, o_ref):
    ts = TOKEN_SUB_WIDE

    def body(r0, sub):
        rows = pl.ds(r0, ts)
        ncs = ts // CHUNK
        y = jax.nn.gelu(jnp.concatenate([yt_ref[:, sub * ncs + c, :] for c in range(ncs)], axis=1))
        gl = _dot(wglut_ref[...], y.astype(BF16))
        ya = (y * jax.nn.sigmoid(gl)).T.astype(BF16)
        merged = ga_ref[rows, :].astype(F32) * _dot(ya, wpa_ref[...]) + pb_ref[rows, :].astype(F32)
        h = x_ref[rows, :] + _dot(merged.astype(BF16), wout_ref[...])

        hn = _rms(h, g_ref[...]).astype(BF16)
        q = _dot(hn, wq_ref[...]).astype(BF16)
        hd = q.shape[1] // XATTN_HEADS
        outs = []
        for hh in range(XATTN_HEADS):
            sl = slice(hh * hd, (hh + 1) * hd)
            s = _dot_nt(q[:, sl], k_ref[0, :, sl])
            e = jnp.exp(s - jnp.max(s, axis=-1, keepdims=True))
            p = e * (1.0 / jnp.sum(e, axis=-1, keepdims=True))
            outs.append(_dot(p.astype(BF16), v_ref[0, :, sl]))
        o = jnp.concatenate(outs, axis=1).astype(BF16)
        o_ref[rows, :] = h + _dot(o, wo_ref[...])

    _sub_loop(x_ref.shape[0], body, ts)


def _merge_xattn(yt3, ga, pb, x2, wglut, wpa, wout, g, wq, k, v, wo):
    t, d = x2.shape
    sw = yt3.shape[0]
    nb, m = k.shape[0], k.shape[1]
    tm = TOKEN_BLOCK
    per_batch = t // nb // tm
    tok = lambda i: (i, 0)
    mem = lambda i: (i // per_batch, 0, 0)
    consts = (wglut, wpa, wout, g, wq)
    return pl.pallas_call(
        _merge_xattn_kernel,
        grid=(t // tm,),
        in_specs=[pl.BlockSpec((sw, tm // CHUNK, CHUNK), lambda i: (0, i, 0)),
                  pl.BlockSpec((tm, d), tok), pl.BlockSpec((tm, d), tok), pl.BlockSpec((tm, d), tok)]
                 + [_const_spec(c.shape) for c in consts]
                 + [pl.BlockSpec((1, m, d), mem), pl.BlockSpec((1, m, d), mem), _const_spec(wo.shape)],
        out_specs=pl.BlockSpec((tm, d), tok),
        out_shape=jax.ShapeDtypeStruct((t, d), F32),
        compiler_params=_cparams("arbitrary"),
        name="merge_xattn",
    )(yt3, ga, pb, x2, *consts, k, v, wo)


def _ffn_kernel(h_ref, g_ref, wg_ref, wu_ref, wd_ref, gf_ref, o_ref):
    def body(r0, sub):
        rows = pl.ds(r0, TOKEN_SUB)
        h = h_ref[rows, :]
        hn = _rms(h, g_ref[...]).astype(BF16)
        act = (jax.nn.silu(_dot(hn, wg_ref[...])) * _dot(hn, wu_ref[...])).astype(BF16)
        o_ref[rows, :] = _rms(h + _dot(act, wd_ref[...]), gf_ref[...])

    _sub_loop(h_ref.shape[0], body)


def _ffn(h2, g, wg, wu, wd, gf):
    t, d = h2.shape
    tm = TOKEN_BLOCK
    consts = (g, wg, wu, wd, gf)
    return pl.pallas_call(
        _ffn_kernel,
        grid=(t // tm,),
        in_specs=[pl.BlockSpec((tm, d), lambda i: (i, 0))] + [_const_spec(c.shape) for c in consts],
        out_specs=pl.BlockSpec((tm, d), lambda i: (i, 0)),
        out_shape=jax.ShapeDtypeStruct((t, d), F32),
        compiler_params=_cparams("arbitrary"),
        name="ffn",
    )(h2, *consts)


def _layer(h, kmem, vmem, mix_g, w_in, s5p, w_glu, ln_g, ln_b, sgu_w, sgu_bias, wpa, wpb, wout,
           xg, wq, wxo, fg, wgate, wup, wdown, final_g):
    b, l, d = h.shape
    t = b * l
    sw = w_glu.shape[0]
    su = ln_g.shape[0]
    row = lambda a: a.astype(F32).reshape(1, -1)
    x2 = h.reshape(t, d)
    bias_full = jnp.repeat(sgu_bias.astype(F32).T, su // SGU_HEADS, axis=1)
    xat3, ga, pb = _in_proj(
        x2, row(mix_g), w_in[:, :sw].T.astype(BF16), w_in.astype(BF16),
        row(ln_g), row(ln_b), sgu_w.astype(BF16), bias_full, wpb.astype(BF16))
    yt3 = _s5(xat3, _s5_tables(*s5p), b)
    h2 = _merge_xattn(yt3, ga, pb, x2, w_glu.T.astype(BF16), wpa.astype(BF16), wout.astype(BF16),
                      row(xg), wq.astype(BF16), kmem, vmem, wxo.astype(BF16))
    out = _ffn(h2, row(fg), wgate.astype(BF16), wup.astype(BF16), wdown.astype(BF16),
               row(final_g))
    return out.reshape(b, l, d)


def kernel(x, mem, mix_norm_g, w_in, s5_lam_re, s5_lam_im, s5_log_step, s5_b_re, s5_b_im, s5_c_re, s5_c_im, s5_d, s5_w_glu, sgu_ln_g, sgu_ln_b, sgu_w, sgu_bias, w_proj_a, w_proj_b, w_out, xattn_norm_g, mem_norm_g, w_q, w_k, w_v, w_xo, ffn_norm_g, w_gate, w_up, w_down, final_norm_g):
    depth = w_in.shape[0]
    assert depth == 1, "the final rms_norm is fused into the last layer's ffn kernel"
    hd = x.shape[-1] // XATTN_HEADS
    i = 0
    kmem, vmem = _mem_kv(mem, mem_norm_g.astype(F32).reshape(1, -1), w_k[i].astype(BF16),
                         w_v[i].astype(BF16), hd ** -0.5)
    s5p = (s5_lam_re[i], s5_lam_im[i], s5_log_step[i], s5_b_re[i], s5_b_im[i], s5_c_re[i],
           s5_c_im[i], s5_d[i])
    return _layer(x, kmem, vmem, mix_norm_g[i], w_in[i], s5p, s5_w_glu[i], sgu_ln_g[i], sgu_ln_b[i],
                  sgu_w[i], sgu_bias[i], w_proj_a[i], w_proj_b[i], w_out[i], xattn_norm_g[i], w_q[i],
                  w_xo[i], ffn_norm_g[i], w_gate[i], w_up[i], w_down[i], final_norm_g)
```

```python
import functools
from typing import NamedTuple

import jax
import jax.numpy as jnp
from jax import lax
from jax.experimental import pallas as pl
from jax.experimental.pallas import tpu as pltpu

F32 = jnp.float32
BF16 = jnp.bfloat16

EPS = 1e-6
LANES = 128
CHUNK = 128
S5_GROUP = 16
S5_STATE = 64
SGU_HEADS = 8
XATTN_HEADS = 4
VMEM_LIMIT = 56 * 1024 * 1024
TOKEN_BLOCK = 1024
TOKEN_SUB = 256
TOKEN_SUB_WIDE = 1024


def _cparams(*sem):
    return pltpu.CompilerParams(dimension_semantics=sem, vmem_limit_bytes=VMEM_LIMIT)


def _const_spec(shape):
    nd = len(shape)
    return pl.BlockSpec(shape, lambda *_: (0,) * nd, pipeline_mode=pl.Buffered(1))


def _rms(x, g):
    return x * lax.rsqrt(jnp.mean(x * x, axis=-1, keepdims=True) + EPS) * g


def _dot(a, b):
    return jnp.dot(a, b, preferred_element_type=F32)


def _dot_nt(a, b):
    return lax.dot_general(a, b, (((1,), (1,)), ((), ())), preferred_element_type=F32)


def _sub_loop(n_tokens, body, ts=TOKEN_SUB):
    def step(sub, carry):
        body(pl.multiple_of(sub * ts, ts), sub)
        return carry
    lax.fori_loop(0, n_tokens // ts, step, 0, unroll=2)


def _mem_kv_kernel(mem_ref, g_ref, wk_ref, wv_ref, k_ref, v_ref, *, scale):
    mn = _rms(mem_ref[0], g_ref[...]).astype(BF16)
    k_ref[0] = (_dot(mn, wk_ref[...]) * scale).astype(BF16)
    v_ref[0] = _dot(mn, wv_ref[...]).astype(BF16)


def _mem_kv(mem, g, wk, wv, scale):
    b, m, d = mem.shape
    return pl.pallas_call(
        functools.partial(_mem_kv_kernel, scale=scale),
        grid=(b,),
        in_specs=[pl.BlockSpec((1, m, d), lambda i: (i, 0, 0)),
                  _const_spec((1, d)), _const_spec((d, d)), _const_spec((d, d))],
        out_specs=[pl.BlockSpec((1, m, d), lambda i: (i, 0, 0))] * 2,
        out_shape=[jax.ShapeDtypeStruct((b, m, d), BF16)] * 2,
        compiler_params=_cparams("arbitrary"),
        name="mem_kv",
    )(mem, g, wk, wv)


def _in_proj_kernel(x_ref, g_ref, wat_ref, win_ref, lng_ref, lnb_ref,
                    ws_ref, bias_ref, wpb_ref, xat_ref, ga_ref, pb_ref):
    def body(r0, sub):
        rows = pl.ds(r0, TOKEN_SUB)
        n = _rms(x_ref[rows, :], g_ref[...]).astype(BF16)
        xat = _dot_nt(wat_ref[...], n)
        for c in range(TOKEN_SUB // CHUNK):
            xat_ref[:, sub * (TOKEN_SUB // CHUNK) + c, :] = xat[:, c * CHUNK:(c + 1) * CHUNK]
        su = lng_ref.shape[1]
        proj = _dot(n, win_ref[:, wat_ref.shape[0]:])
        ga_ref[rows, :] = jax.nn.sigmoid(proj[:, 2 * su:2 * su + ga_ref.shape[1]]).astype(BF16)
        zv = jax.nn.gelu(proj[:, su:2 * su])
        zc = zv - jnp.mean(zv, axis=-1, keepdims=True)
        zvn = (zc * lax.rsqrt(jnp.mean(zc * zc, axis=-1, keepdims=True) + EPS) * lng_ref[...]
               + lnb_ref[...]).astype(BF16)
        hd = zvn.shape[1] // SGU_HEADS
        ncs = TOKEN_SUB // CHUNK
        mixed = [_dot(ws_ref[h], jnp.concatenate(
            [zvn[c * CHUNK:(c + 1) * CHUNK, h * hd:(h + 1) * hd] for c in range(ncs)], axis=1))
            for h in range(SGU_HEADS)]
        sv = jnp.concatenate(
            [jnp.concatenate([m[:, c * hd:(c + 1) * hd] for m in mixed], axis=1) + bias_ref[...]
             for c in range(ncs)], axis=0)
        zu = jax.nn.gelu(proj[:, :su])
        yb = (zu * sv).astype(BF16)
        gb = jax.nn.sigmoid(proj[:, 2 * su + ga_ref.shape[1]:])
        pb_ref[rows, :] = (gb * _dot(yb, wpb_ref[...])).astype(BF16)

    _sub_loop(x_ref.shape[0], body)


def _in_proj(x2, g, wat, win, lng, lnb, ws, bias_full, wpb):
    t, d = x2.shape
    sw = wat.shape[0]
    tm = TOKEN_BLOCK
    consts = (g, wat, win, lng, lnb, ws, bias_full, wpb)
    return pl.pallas_call(
        _in_proj_kernel,
        grid=(t // tm,),
        in_specs=[pl.BlockSpec((tm, d), lambda i: (i, 0))] + [_const_spec(c.shape) for c in consts],
        out_specs=[pl.BlockSpec((sw, tm // CHUNK, CHUNK), lambda i: (0, i, 0)),
                   pl.BlockSpec((tm, d), lambda i: (i, 0)),
                   pl.BlockSpec((tm, d), lambda i: (i, 0))],
        out_shape=[jax.ShapeDtypeStruct((sw, t // CHUNK, CHUNK), F32),
                   jax.ShapeDtypeStruct((t, d), BF16),
                   jax.ShapeDtypeStruct((t, d), BF16)],
        compiler_params=_cparams("arbitrary"),
        name="in_proj",
    )(x2, *consts)


def _cmul(ar, ai, br, bi):
    return ar * br - ai * bi, ar * bi + ai * br


def _cpow(br, bi, e, nbits):
    rr = ri = None
    for k in range(nbits):
        bit = ((e >> k) & 1) == 1
        sr, si = jnp.where(bit, br, 1.0), jnp.where(bit, bi, 0.0)
        rr, ri = (sr, si) if rr is None else _cmul(rr, ri, sr, si)
        if k + 1 < nbits:
            br, bi = _cmul(br, bi, br, bi)
    return rr, ri


class _S5Tabs(NamedTuple):
    kf: tuple
    kb: tuple
    lt: tuple
    et: tuple
    bb: tuple
    cc: tuple
    a: tuple


def _s5_tables_in_kernel(row_ref, col_ref, bc_ref):
    gc, half, ns = S5_GROUP, S5_GROUP // 2, 2 * S5_STATE
    sq = (CHUNK, CHUNK)
    sub = lax.broadcasted_iota(jnp.int32, sq, 0)
    lan = lax.broadcasted_iota(jnp.int32, sq, 1)
    lb_re, lb_im = row_ref[0, 0:1, :], row_ref[0, 1:2, :]
    q_re, q_im = row_ref[0, 2:3, :], row_ref[0, 3:4, :]
    fwd_lane = lan < S5_STATE
    lt = _cpow(lb_re, lb_im, jnp.where(fwd_lane, CHUNK - 1 - sub, sub), 7)
    et = _cpow(lb_re, lb_im, jnp.where(fwd_lane, sub + 1, CHUNK - sub), 8)
    p_re, p_im = _cpow(col_ref[0, :, 0:1], col_ref[0, :, 1:2],
                       jnp.where(sub < S5_STATE, lan, (CHUNK - lan) & (CHUNK - 1)), 7)
    a_re, a_im = lb_re, lb_im
    for _ in range(7):
        a_re, a_im = _cmul(a_re, a_im, a_re, a_im)
    bb = _cmul(q_re, q_im, bc_ref[0, 0], bc_ref[0, 1])
    c_re, c_im = bc_ref[0, 2], bc_ref[0, 3]

    rhs_f = jnp.concatenate([p_re[:S5_STATE], p_im[:S5_STATE]], axis=0)
    rhs_b = jnp.concatenate([p_re[S5_STATE:], p_im[S5_STATE:]], axis=0)
    lane_f = lax.broadcasted_iota(jnp.int32, (gc * half, ns), 1) < S5_STATE
    lane_tap = lax.broadcasted_iota(jnp.int32, (gc * half, CHUNK), 1)
    taps = []
    for par in range(2):
        cpr, cpi = c_re[par * half:(par + 1) * half], c_im[par * half:(par + 1) * half]
        cb = [_cmul(cpr, cpi, bb[0][ci:ci + 1], bb[1][ci:ci + 1]) for ci in range(gc)]
        cb_re = jnp.concatenate([v[0] for v in cb], axis=0)
        cb_im = jnp.concatenate([v[1] for v in cb], axis=0)
        lhs_f = jnp.where(lane_f, cb_re, -pltpu.roll(cb_im, S5_STATE, 1))
        lhs_b = jnp.where(lane_f, pltpu.roll(cb_re, S5_STATE, 1), -cb_im)
        kf = jnp.dot(lhs_f, rhs_f, preferred_element_type=F32, precision=lax.Precision.HIGHEST)
        kb = jnp.dot(lhs_b, rhs_b, preferred_element_type=F32, precision=lax.Precision.HIGHEST)
        kf = kf + jnp.where(lane_tap == 0, kb[:, 0:1] + col_ref[0, :, 2 + par:3 + par], 0.0)
        taps.append((kf, kb))

    return _S5Tabs((taps[0][0], taps[1][0]), (taps[0][1], taps[1][1]),
                   lt, et, bb, (c_re, c_im), (a_re, a_im))


def _s5_build_small(tabs, slot, bm_ref, cm_ref, a_ref):
    half = S5_GROUP // 2
    for ch in range(S5_GROUP):
        w_re, w_im = _cmul(*tabs.lt, tabs.bb[0][ch:ch + 1], tabs.bb[1][ch:ch + 1])
        bm_ref[slot, ch * CHUNK:(ch + 1) * CHUNK, :] = jnp.concatenate([w_re, w_im], axis=1).astype(BF16)
        r = (ch % 2) * half + ch // 2
        g_re, g_im = _cmul(*tabs.et, tabs.cc[0][r:r + 1], tabs.cc[1][r:r + 1])
        cm_ref[slot, ch * CHUNK:(ch + 1) * CHUNK, :] = jnp.concatenate([g_re, -g_im], axis=1).astype(BF16)
    a_ref[slot, 0] = jnp.broadcast_to(tabs.a[0], a_ref.shape[2:])
    a_ref[slot, 1] = jnp.broadcast_to(tabs.a[1], a_ref.shape[2:])


def _s5_build_pair(tabs, slot, q, m_ref):
    half = S5_GROUP // 2
    sq = (CHUNK, CHUNK)
    upper = lax.broadcasted_iota(jnp.int32, sq, 1) >= lax.broadcasted_iota(jnp.int32, sq, 0)
    for ci in range(S5_GROUP):
        r = ci * half + q
        blocks = []
        for par in range(2):
            vf = jnp.broadcast_to(tabs.kf[par][r:r + 1], sq)
            vb = jnp.broadcast_to(tabs.kb[par][r:r + 1], sq)
            cf = pltpu.roll(vf, 0, 1, stride=1, stride_axis=0)
            cbw = pltpu.roll(vb, 0, 1, stride=1, stride_axis=0)
            blocks.append(jnp.where(upper, cf, cbw))
        m_ref[slot, ci * CHUNK:(ci + 1) * CHUNK, 2 * q * CHUNK:(2 * q + 2) * CHUNK] = (
            jnp.concatenate(blocks, axis=1).astype(BF16))


def _s5_apply(x_ref, y_ref, gi, slot, m_ref, bm_ref, cm_ref, a_ref, s_ref, xf_ref, xb_ref, nbatch,
              between):
    gc, ns = S5_GROUP, 2 * S5_STATE
    nrows = x_ref.shape[1]
    nch = nrows // nbatch
    x = jnp.concatenate([x_ref[gi * gc + ci].astype(BF16) for ci in range(gc)], axis=1)
    s = _dot(x, bm_ref[slot])
    s_ref[0] = s[:, :ns]
    s_ref[1] = s[:, ns:]
    a_re, a_im = a_ref[slot, 0, 0:1, :], a_ref[slot, 1, 0:1, :]
    is_fwd = lax.broadcasted_iota(jnp.int32, (nbatch, ns), 1) < S5_STATE
    st_re = st_im = jnp.zeros((nbatch, ns), F32)
    for i in range(nch):
        rows_f = pl.ds(i, nbatch, stride=nch)
        rows_b = pl.ds(nch - 1 - i, nbatch, stride=nch)
        xf_ref[0, rows_f, :] = st_re
        xf_ref[1, rows_f, :] = st_im
        xb_ref[0, rows_b, :] = st_re
        xb_ref[1, rows_b, :] = st_im
        if i + 1 < nch:
            s_re = jnp.where(is_fwd, s_ref[0, rows_f, :], s_ref[0, rows_b, :])
            s_im = jnp.where(is_fwd, s_ref[1, rows_f, :], s_ref[1, rows_b, :])
            n_re, n_im = _cmul(a_re, a_im, st_re, st_im)
            st_re, st_im = n_re + s_re, n_im + s_im
    is_fwd_all = lax.broadcasted_iota(jnp.int32, (nrows, ns), 1) < S5_STATE
    xin = jnp.concatenate([jnp.where(is_fwd_all, xf_ref[0], xb_ref[0]),
                           jnp.where(is_fwd_all, xf_ref[1], xb_ref[1])], axis=1).astype(BF16)
    for q in range(gc // 2):
        between(q)
        cols = slice(2 * q * CHUNK, (2 * q + 2) * CHUNK)
        y = jax.nn.gelu(_dot(x, m_ref[slot, :, cols]) + _dot_nt(xin, cm_ref[slot, cols, :]))
        y_ref[gi * gc + 2 * q] = y[:, :CHUNK]
        y_ref[gi * gc + 2 * q + 1] = y[:, CHUNK:]


def _s5_kernel(x_ref, row0, col0, bc0, row1, col1, bc1,
               y_ref, m_ref, bm_ref, cm_ref, a_ref, s_ref, xf_ref, xb_ref, *, nbatch):
    scr = (m_ref, bm_ref, cm_ref, a_ref, s_ref, xf_ref, xb_ref)
    k = pl.program_id(0)

    @pl.when(k == 0)
    def _():
        tabs = _s5_tables_in_kernel(row0, col0, bc0)
        _s5_build_small(tabs, 0, bm_ref, cm_ref, a_ref)
        for q in range(S5_GROUP // 2):
            _s5_build_pair(tabs, 0, q, m_ref)

    for cur in range(2):
        @pl.when(k % 2 == cur)
        def _():
            tabs = _s5_tables_in_kernel(row1, col1, bc1)
            _s5_build_small(tabs, 1 - cur, bm_ref, cm_ref, a_ref)
            _s5_apply(x_ref, y_ref, 0, cur, *scr, nbatch,
                      functools.partial(_s5_build_pair, tabs, 1 - cur, m_ref=m_ref))


def _s5(xat3, tabs, nbatch):
    sw, nrows, _ = xat3.shape
    gc, ns = S5_GROUP, 2 * S5_STATE
    nsteps = sw // gc
    last = nsteps - 1

    def tspecs(index):
        return [pl.BlockSpec((1,) + a.shape[1:], lambda k, n=a.ndim: (index(k),) + (0,) * (n - 1))
                for a in tabs]

    return pl.pallas_call(
        functools.partial(_s5_kernel, nbatch=nbatch),
        grid=(nsteps,),
        in_specs=[pl.BlockSpec((gc, nrows, CHUNK), lambda k: (k, 0, 0))]
                 + tspecs(lambda k: 0) + tspecs(lambda k: jnp.minimum(k + 1, last)),
        out_specs=pl.BlockSpec((gc, nrows, CHUNK), lambda k: (k, 0, 0)),
        out_shape=jax.ShapeDtypeStruct((sw, nrows, CHUNK), F32),
        scratch_shapes=[pltpu.VMEM((2, gc * CHUNK, gc * CHUNK), BF16),
                        pltpu.VMEM((2, gc * CHUNK, 2 * ns), BF16),
                        pltpu.VMEM((2, gc * CHUNK, 2 * ns), BF16),
                        pltpu.VMEM((2, 2, 8, ns), F32),
                        pltpu.VMEM((2, nrows, ns), F32),
                        pltpu.VMEM((2, nrows, ns), F32),
                        pltpu.VMEM((2, nrows, ns), F32)],
        compiler_params=_cparams("arbitrary"),
        name="s5",
    )(xat3, *tabs, *tabs)


def _s5_tables(lam_re, lam_im, log_step, b_re, b_im, c_re, c_im, d):
    ng, npar, nc = b_re.shape[1], b_re.shape[2], b_re.shape[3]
    half = nc // 2
    lam_re, lam_im = lam_re.astype(F32), lam_im.astype(F32)
    step = jnp.exp(log_step.astype(F32))[..., None]
    mag = jnp.exp(lam_re * step)
    lb_re, lb_im = mag * jnp.cos(lam_im * step), mag * jnp.sin(lam_im * step)
    den = lam_re * lam_re + lam_im * lam_im
    q_re = ((lb_re - 1.0) * lam_re + lb_im * lam_im) / den
    q_im = (lb_im * lam_re - (lb_re - 1.0) * lam_im) / den
    lanes = lambda a: a.transpose(1, 0, 2).reshape(ng, 2 * npar)
    rows = jnp.stack([lanes(a) for a in (lb_re, lb_im, q_re, q_im)], axis=1)
    rows = jnp.pad(rows, ((0, 0), (0, 4), (0, 0)))
    ci = jnp.arange(nc)[:, None, None]
    co = 2 * jnp.arange(half)[None, :, None] + jnp.arange(2)[None, None, :]
    dsel = (ci == co).astype(F32).reshape(nc * half, 2)
    dcol = jnp.repeat(d.astype(F32).reshape(ng, nc), half, axis=1)[..., None] * dsel[None]
    cols = jnp.concatenate([rows[:, :2].transpose(0, 2, 1), dcol], axis=2)
    perm = jnp.concatenate([jnp.arange(0, nc, 2), jnp.arange(1, nc, 2)])
    bc = jnp.stack([b_re.astype(F32).transpose(1, 3, 0, 2).reshape(ng, nc, 2 * npar),
                    b_im.astype(F32).transpose(1, 3, 0, 2).reshape(ng, nc, 2 * npar),
                    c_re.astype(F32).transpose(1, 2, 0, 3).reshape(ng, nc, 2 * npar)[:, perm],
                    c_im.astype(F32).transpose(1, 2, 0, 3).reshape(ng, nc, 2 * npar)[:, perm]],
                   axis=1)
    return rows, cols, bc


def _merge_xattn_kernel(yt_ref, ga_ref, pb_ref, x_ref, wglut_ref, wpa_ref, wout_ref,
                        g_ref, wq_ref, k_ref, v_ref, wo_ref, o_ref):
    ts = TOKEN_SUB_WIDE

    def body(r0, sub):
        rows = pl.ds(r0, ts)
        ncs = ts // CHUNK
        y = jnp.concatenate([yt_ref[:, sub * ncs + c, :] for c in range(ncs)], axis=1)
        gl = _dot(wglut_ref[...], y.astype(BF16))
        ya = (y * jax.nn.sigmoid(gl)).T.astype(BF16)
        merged = ga_ref[rows, :].astype(F32) * _dot(ya, wpa_ref[...]) + pb_ref[rows, :].astype(F32)
        h = x_ref[rows, :] + _dot(merged.astype(BF16), wout_ref[...])

        hn = _rms(h, g_ref[...]).astype(BF16)
        q = _dot(hn, wq_ref[...]).astype(BF16)
        hd = q.shape[1] // XATTN_HEADS
        outs = []
        for hh in range(XATTN_HEADS):
            sl = slice(hh * hd, (hh + 1) * hd)
            s = _dot_nt(q[:, sl], k_ref[0, :, sl])
            e = jnp.exp(s - jnp.max(s, axis=-1, keepdims=True))
            p = e * (1.0 / jnp.sum(e, axis=-1, keepdims=True))
            outs.append(_dot(p.astype(BF16), v_ref[0, :, sl]))
        o = jnp.concatenate(outs, axis=1).astype(BF16)
        o_ref[rows, :] = h + _dot(o, wo_ref[...])

    _sub_loop(x_ref.shape[0], body, ts)


def _merge_xattn(yt3, ga, pb, x2, wglut, wpa, wout, g, wq, k, v, wo):
    t, d = x2.shape
    sw = yt3.shape[0]
    nb, m = k.shape[0], k.shape[1]
    tm = TOKEN_BLOCK
    per_batch = t // nb // tm
    tok = lambda i: (i, 0)
    mem = lambda i: (i // per_batch, 0, 0)
    consts = (wglut, wpa, wout, g, wq)
    return pl.pallas_call(
        _merge_xattn_kernel,
        grid=(t // tm,),
        in_specs=[pl.BlockSpec((sw, tm // CHUNK, CHUNK), lambda i: (0, i, 0)),
                  pl.BlockSpec((tm, d), tok), pl.BlockSpec((tm, d), tok), pl.BlockSpec((tm, d), tok)]
                 + [_const_spec(c.shape) for c in consts]
                 + [pl.BlockSpec((1, m, d), mem), pl.BlockSpec((1, m, d), mem), _const_spec(wo.shape)],
        out_specs=pl.BlockSpec((tm, d), tok),
        out_shape=jax.ShapeDtypeStruct((t, d), F32),
        compiler_params=_cparams("arbitrary"),
        name="merge_xattn",
    )(yt3, ga, pb, x2, *consts, k, v, wo)


def _ffn_kernel(h_ref, g_ref, wg_ref, wu_ref, wd_ref, gf_ref, o_ref):
    def body(r0, sub):
        rows = pl.ds(r0, TOKEN_SUB)
        h = h_ref[rows, :]
        hn = _rms(h, g_ref[...]).astype(BF16)
        act = (jax.nn.silu(_dot(hn, wg_ref[...])) * _dot(hn, wu_ref[...])).astype(BF16)
        o_ref[rows, :] = _rms(h + _dot(act, wd_ref[...]), gf_ref[...])

    _sub_loop(h_ref.shape[0], body)


def _ffn(h2, g, wg, wu, wd, gf):
    t, d = h2.shape
    tm = TOKEN_BLOCK
    consts = (g, wg, wu, wd, gf)
    return pl.pallas_call(
        _ffn_kernel,
        grid=(t // tm,),
        in_specs=[pl.BlockSpec((tm, d), lambda i: (i, 0))] + [_const_spec(c.shape) for c in consts],
        out_specs=pl.BlockSpec((tm, d), lambda i: (i, 0)),
        out_shape=jax.ShapeDtypeStruct((t, d), F32),
        compiler_params=_cparams("arbitrary"),
        name="ffn",
    )(h2, *consts)


def _layer(h, kmem, vmem, mix_g, w_in, s5p, w_glu, ln_g, ln_b, sgu_w, sgu_bias, wpa, wpb, wout,
           xg, wq, wxo, fg, wgate, wup, wdown, final_g):
    b, l, d = h.shape
    t = b * l
    sw = w_glu.shape[0]
    su = ln_g.shape[0]
    row = lambda a: a.astype(F32).reshape(1, -1)
    x2 = h.reshape(t, d)
    bias_full = jnp.repeat(sgu_bias.astype(F32).T, su // SGU_HEADS, axis=1)
    xat3, ga, pb = _in_proj(
        x2, row(mix_g), w_in[:, :sw].T.astype(BF16), w_in.astype(BF16),
        row(ln_g), row(ln_b), sgu_w.astype(BF16), bias_full, wpb.astype(BF16))
    yt3 = _s5(xat3, _s5_tables(*s5p), b)
    h2 = _merge_xattn(yt3, ga, pb, x2, w_glu.T.astype(BF16), wpa.astype(BF16), wout.astype(BF16),
                      row(xg), wq.astype(BF16), kmem, vmem, wxo.astype(BF16))
    out = _ffn(h2, row(fg), wgate.astype(BF16), wup.astype(BF16), wdown.astype(BF16),
               row(final_g))
    return out.reshape(b, l, d)


def kernel(x, mem, mix_norm_g, w_in, s5_lam_re, s5_lam_im, s5_log_step, s5_b_re, s5_b_im, s5_c_re, s5_c_im, s5_d, s5_w_glu, sgu_ln_g, sgu_ln_b, sgu_w, sgu_bias, w_proj_a, w_proj_b, w_out, xattn_norm_g, mem_norm_g, w_q, w_k, w_v, w_xo, ffn_norm_g, w_gate, w_up, w_down, final_norm_g):
    depth = w_in.shape[0]
    assert depth == 1, "the final rms_norm is fused into the last layer's ffn kernel"
    hd = x.shape[-1] // XATTN_HEADS
    i = 0
    kmem, vmem = _mem_kv(mem, mem_norm_g.astype(F32).reshape(1, -1), w_k[i].astype(BF16),
                         w_v[i].astype(BF16), hd ** -0.5)
    s5p = (s5_lam_re[i], s5_lam_im[i], s5_log_step[i], s5_b_re[i], s5_b_im[i], s5_c_re[i],
           s5_c_im[i], s5_d[i])
    return _layer(x, kmem, vmem, mix_norm_g[i], w_in[i], s5p, s5_w_glu[i], sgu_ln_g[i], sgu_ln_b[i],
                  sgu_w[i], sgu_bias[i], w_proj_a[i], w_proj_b[i], w_out[i], xattn_norm_g[i], w_q[i],
                  w_xo[i], ffn_norm_g[i], w_gate[i], w_up[i], w_down[i], final_norm_g)
```

```python
import functools
from typing import NamedTuple

import jax
import jax.numpy as jnp
from jax import lax
from jax.experimental import pallas as pl
from jax.experimental.pallas import tpu as pltpu

F32 = jnp.float32
BF16 = jnp.bfloat16

EPS = 1e-6
LANES = 128
CHUNK = 128
S5_GROUP = 16
S5_STATE = 64
SGU_HEADS = 8
XATTN_HEADS = 4
VMEM_LIMIT = 56 * 1024 * 1024
TOKEN_BLOCK = 1024
TOKEN_SUB = 256
TOKEN_SUB_WIDE = 1024


def _cparams(*sem):
    return pltpu.CompilerParams(dimension_semantics=sem, vmem_limit_bytes=VMEM_LIMIT)


def _const_spec(shape):
    nd = len(shape)
    return pl.BlockSpec(shape, lambda *_: (0,) * nd, pipeline_mode=pl.Buffered(1))


def _rms(x, g):
    return x * lax.rsqrt(jnp.mean(x * x, axis=-1, keepdims=True) + EPS) * g


def _dot(a, b):
    return jnp.dot(a, b, preferred_element_type=F32)


def _dot_nt(a, b):
    return lax.dot_general(a, b, (((1,), (1,)), ((), ())), preferred_element_type=F32)


def _sub_loop(n_tokens, body, ts=TOKEN_SUB):
    def step(sub, carry):
        body(pl.multiple_of(sub * ts, ts), sub)
        return carry
    lax.fori_loop(0, n_tokens // ts, step, 0, unroll=2)


def _mem_kv_kernel(mem_ref, g_ref, wk_ref, wv_ref, k_ref, v_ref, *, scale):
    mn = _rms(mem_ref[0], g_ref[...]).astype(BF16)
    k_ref[0] = (_dot(mn, wk_ref[...]) * scale).astype(BF16)
    v_ref[0] = _dot(mn, wv_ref[...]).astype(BF16)


def _mem_kv(mem, g, wk, wv, scale):
    b, m, d = mem.shape
    return pl.pallas_call(
        functools.partial(_mem_kv_kernel, scale=scale),
        grid=(b,),
        in_specs=[pl.BlockSpec((1, m, d), lambda i: (i, 0, 0)),
                  _const_spec((1, d)), _const_spec((d, d)), _const_spec((d, d))],
        out_specs=[pl.BlockSpec((1, m, d), lambda i: (i, 0, 0))] * 2,
        out_shape=[jax.ShapeDtypeStruct((b, m, d), BF16)] * 2,
        compiler_params=_cparams("arbitrary"),
        name="mem_kv",
    )(mem, g, wk, wv)


def _in_proj_kernel(x_ref, g_ref, wat_ref, win_ref, lng_ref, lnb_ref,
                    ws_ref, bias_ref, wpb_ref, xat_ref, ga_ref, pb_ref):
    def body(r0, sub):
        rows = pl.ds(r0, TOKEN_SUB)
        n = _rms(x_ref[rows, :], g_ref[...]).astype(BF16)
        xat = _dot_nt(wat_ref[...], n)
        for c in range(TOKEN_SUB // CHUNK):
            xat_ref[:, sub * (TOKEN_SUB // CHUNK) + c, :] = xat[:, c * CHUNK:(c + 1) * CHUNK]
        su = lng_ref.shape[1]
        proj = _dot(n, win_ref[:, wat_ref.shape[0]:])
        ga_ref[rows, :] = jax.nn.sigmoid(proj[:, 2 * su:2 * su + ga_ref.shape[1]]).astype(BF16)
        zv = jax.nn.gelu(proj[:, su:2 * su])
        zc = zv - jnp.mean(zv, axis=-1, keepdims=True)
        zvn = (zc * lax.rsqrt(jnp.mean(zc * zc, axis=-1, keepdims=True) + EPS) * lng_ref[...]
               + lnb_ref[...]).astype(BF16)
        hd = zvn.shape[1] // SGU_HEADS
        ncs = TOKEN_SUB // CHUNK
        mixed = [_dot(ws_ref[h], jnp.concatenate(
            [zvn[c * CHUNK:(c + 1) * CHUNK, h * hd:(h + 1) * hd] for c in range(ncs)], axis=1))
            for h in range(SGU_HEADS)]
        sv = jnp.concatenate(
            [jnp.concatenate([m[:, c * hd:(c + 1) * hd] for m in mixed], axis=1) + bias_ref[...]
             for c in range(ncs)], axis=0)
        zu = jax.nn.gelu(proj[:, :su])
        yb = (zu * sv).astype(BF16)
        gb = jax.nn.sigmoid(proj[:, 2 * su + ga_ref.shape[1]:])
        pb_ref[rows, :] = (gb * _dot(yb, wpb_ref[...])).astype(BF16)

    _sub_loop(x_ref.shape[0], body)


def _in_proj(x2, g, wat, win, lng, lnb, ws, bias_full, wpb):
    t, d = x2.shape
    sw = wat.shape[0]
    tm = TOKEN_BLOCK
    consts = (g, wat, win, lng, lnb, ws, bias_full, wpb)
    return pl.pallas_call(
        _in_proj_kernel,
        grid=(t // tm,),
        in_specs=[pl.BlockSpec((tm, d), lambda i: (i, 0))] + [_const_spec(c.shape) for c in consts],
        out_specs=[pl.BlockSpec((sw, tm // CHUNK, CHUNK), lambda i: (0, i, 0)),
                   pl.BlockSpec((tm, d), lambda i: (i, 0)),
                   pl.BlockSpec((tm, d), lambda i: (i, 0))],
        out_shape=[jax.ShapeDtypeStruct((sw, t // CHUNK, CHUNK), F32),
                   jax.ShapeDtypeStruct((t, d), BF16),
                   jax.ShapeDtypeStruct((t, d), BF16)],
        compiler_params=_cparams("arbitrary"),
        name="in_proj",
    )(x2, *consts)


def _cmul(ar, ai, br, bi):
    return ar * br - ai * bi, ar * bi + ai * br


def _cpow(br, bi, e, nbits):
    rr = ri = None
    for k in range(nbits):
        bit = ((e >> k) & 1) == 1
        sr, si = jnp.where(bit, br, 1.0), jnp.where(bit, bi, 0.0)
        rr, ri = (sr, si) if rr is None else _cmul(rr, ri, sr, si)
        if k + 1 < nbits:
            br, bi = _cmul(br, bi, br, bi)
    return rr, ri


class _S5Tabs(NamedTuple):
    kf: tuple
    kb: tuple
    lt: tuple
    et: tuple
    bb: tuple
    cc: tuple
    a: tuple


def _s5_tables_in_kernel(row_ref, col_ref, bc_ref):
    gc, half, ns = S5_GROUP, S5_GROUP // 2, 2 * S5_STATE
    sq = (CHUNK, CHUNK)
    sub = lax.broadcasted_iota(jnp.int32, sq, 0)
    lan = lax.broadcasted_iota(jnp.int32, sq, 1)
    lb_re, lb_im = row_ref[0, 0:1, :], row_ref[0, 1:2, :]
    q_re, q_im = row_ref[0, 2:3, :], row_ref[0, 3:4, :]
    fwd_lane = lan < S5_STATE
    lt = _cpow(lb_re, lb_im, jnp.where(fwd_lane, CHUNK - 1 - sub, sub), 7)
    et = _cpow(lb_re, lb_im, jnp.where(fwd_lane, sub + 1, CHUNK - sub), 8)
    p_re, p_im = _cpow(col_ref[0, :, 0:1], col_ref[0, :, 1:2],
                       jnp.where(sub < S5_STATE, lan, (CHUNK - lan) & (CHUNK - 1)), 7)
    a_re, a_im = lb_re, lb_im
    for _ in range(7):
        a_re, a_im = _cmul(a_re, a_im, a_re, a_im)
    bb = _cmul(q_re, q_im, bc_ref[0, 0], bc_ref[0, 1])
    c_re, c_im = bc_ref[0, 2], bc_ref[0, 3]

    rhs_f = jnp.concatenate([p_re[:S5_STATE], p_im[:S5_STATE]], axis=0)
    rhs_b = jnp.concatenate([p_re[S5_STATE:], p_im[S5_STATE:]], axis=0)
    lane_f = lax.broadcasted_iota(jnp.int32, (gc * half, ns), 1) < S5_STATE
    lane_tap = lax.broadcasted_iota(jnp.int32, (gc * half, CHUNK), 1)
    taps = []
    for par in range(2):
        cpr, cpi = c_re[par * half:(par + 1) * half], c_im[par * half:(par + 1) * half]
        cb = [_cmul(cpr, cpi, bb[0][ci:ci + 1], bb[1][ci:ci + 1]) for ci in range(gc)]
        cb_re = jnp.concatenate([v[0] for v in cb], axis=0)
        cb_im = jnp.concatenate([v[1] for v in cb], axis=0)
        lhs_f = jnp.where(lane_f, cb_re, -pltpu.roll(cb_im, S5_STATE, 1))
        lhs_b = jnp.where(lane_f, pltpu.roll(cb_re, S5_STATE, 1), -cb_im)
        kf = jnp.dot(lhs_f, rhs_f, preferred_element_type=F32, precision=lax.Precision.HIGHEST)
        kb = jnp.dot(lhs_b, rhs_b, preferred_element_type=F32, precision=lax.Precision.HIGHEST)
        kf = kf + jnp.where(lane_tap == 0, kb[:, 0:1] + col_ref[0, :, 2 + par:3 + par], 0.0)
        taps.append((kf, kb))

    return _S5Tabs((taps[0][0], taps[1][0]), (taps[0][1], taps[1][1]),
                   lt, et, bb, (c_re, c_im), (a_re, a_im))


def _s5_build_small(tabs, slot, bm_ref, cm_ref, a_ref):
    half = S5_GROUP // 2
    for ch in range(S5_GROUP):
        w_re, w_im = _cmul(*tabs.lt, tabs.bb[0][ch:ch + 1], tabs.bb[1][ch:ch + 1])
        bm_ref[slot, ch * CHUNK:(ch + 1) * CHUNK, :] = jnp.concatenate([w_re, w_im], axis=1).astype(BF16)
        r = (ch % 2) * half + ch // 2
        g_re, g_im = _cmul(*tabs.et, tabs.cc[0][r:r + 1], tabs.cc[1][r:r + 1])
        cm_ref[slot, ch * CHUNK:(ch + 1) * CHUNK, :] = jnp.concatenate([g_re, -g_im], axis=1).astype(BF16)
    a_ref[slot, 0] = jnp.broadcast_to(tabs.a[0], a_ref.shape[2:])
    a_ref[slot, 1] = jnp.broadcast_to(tabs.a[1], a_ref.shape[2:])


def _s5_build_pair(tabs, slot, q, m_ref):
    half = S5_GROUP // 2
    sq = (CHUNK, CHUNK)
    upper = lax.broadcasted_iota(jnp.int32, sq, 1) >= lax.broadcasted_iota(jnp.int32, sq, 0)
    for ci in range(S5_GROUP):
        r = ci * half + q
        blocks = []
        for par in range(2):
            vf = jnp.broadcast_to(tabs.kf[par][r:r + 1], sq)
            vb = jnp.broadcast_to(tabs.kb[par][r:r + 1], sq)
            cf = pltpu.roll(vf, 0, 1, stride=1, stride_axis=0)
            cbw = pltpu.roll(vb, 0, 1, stride=1, stride_axis=0)
            blocks.append(jnp.where(upper, cf, cbw))
        m_ref[slot, ci * CHUNK:(ci + 1) * CHUNK, 2 * q * CHUNK:(2 * q + 2) * CHUNK] = (
            jnp.concatenate(blocks, axis=1).astype(BF16))


def _s5_apply(x_ref, y_ref, gi, slot, m_ref, bm_ref, cm_ref, a_ref, s_ref, xf_ref, xb_ref, nbatch,
              between):
    gc, ns = S5_GROUP, 2 * S5_STATE
    nrows = x_ref.shape[1]
    nch = nrows // nbatch
    x = jnp.concatenate([x_ref[gi * gc + ci].astype(BF16) for ci in range(gc)], axis=1)
    s = _dot(x, bm_ref[slot])
    s_ref[0] = s[:, :ns]
    s_ref[1] = s[:, ns:]
    a_re, a_im = a_ref[slot, 0, 0:1, :], a_ref[slot, 1, 0:1, :]
    is_fwd = lax.broadcasted_iota(jnp.int32, (nbatch, ns), 1) < S5_STATE
    st_re = st_im = jnp.zeros((nbatch, ns), F32)
    for i in range(nch):
        rows_f = pl.ds(i, nbatch, stride=nch)
        rows_b = pl.ds(nch - 1 - i, nbatch, stride=nch)
        xf_ref[0, rows_f, :] = st_re
        xf_ref[1, rows_f, :] = st_im
        xb_ref[0, rows_b, :] = st_re
        xb_ref[1, rows_b, :] = st_im
        if i + 1 < nch:
            s_re = jnp.where(is_fwd, s_ref[0, rows_f, :], s_ref[0, rows_b, :])
            s_im = jnp.where(is_fwd, s_ref[1, rows_f, :], s_ref[1, rows_b, :])
            n_re, n_im = _cmul(a_re, a_im, st_re, st_im)
            st_re, st_im = n_re + s_re, n_im + s_im
    is_fwd_all = lax.broadcasted_iota(jnp.int32, (nrows, ns), 1) < S5_STATE
    xin = jnp.concatenate([jnp.where(is_fwd_all, xf_ref[0], xb_ref[0]),
                           jnp.where(is_fwd_all, xf_ref[1], xb_ref[1])], axis=1).astype(BF16)
    for q in range(gc // 2):
        between(q)
        cols = slice(2 * q * CHUNK, (2 * q + 2) * CHUNK)
        y = _dot(x, m_ref[slot, :, cols]) + _dot_nt(xin, cm_ref[slot, cols, :])
        y_ref[gi * gc + 2 * q] = y[:, :CHUNK]
        y_ref[gi * gc + 2 * q + 1] = y[:, CHUNK:]


def _s5_kernel(x_ref, row0, col0, bc0, row1, col1, bc1,
               y_ref, m_ref, bm_ref, cm_ref, a_ref, s_ref, xf_ref, xb_ref, *, nbatch):
    scr = (m_ref, bm_ref, cm_ref, a_ref, s_ref, xf_ref, xb_ref)
    k = pl.program_id(0)

    @pl.when(k == 0)
    def _():
        tabs = _s5_tables_in_kernel(row0, col0, bc0)
        _s5_build_small(tabs, 0, bm_ref, cm_ref, a_ref)
        for q in range(S5_GROUP // 2):
            _s5_build_pair(tabs, 0, q, m_ref)

    for cur in range(2):
        @pl.when(k % 2 == cur)
        def _():
            tabs = _s5_tables_in_kernel(row1, col1, bc1)
            _s5_build_small(tabs, 1 - cur, bm_ref, cm_ref, a_ref)
            _s5_apply(x_ref, y_ref, 0, cur, *scr, nbatch,
                      functools.partial(_s5_build_pair, tabs, 1 - cur, m_ref=m_ref))


def _s5(xat3, tabs, nbatch):
    sw, nrows, _ = xat3.shape
    gc, ns = S5_GROUP, 2 * S5_STATE
    nsteps = sw // gc
    last = nsteps - 1

    def tspecs(index):
        return [pl.BlockSpec((1,) + a.shape[1:], lambda k, n=a.ndim: (index(k),) + (0,) * (n - 1))
                for a in tabs]

    return pl.pallas_call(
        functools.partial(_s5_kernel, nbatch=nbatch),
        grid=(nsteps,),
        in_specs=[pl.BlockSpec((gc, nrows, CHUNK), lambda k: (k, 0, 0))]
                 + tspecs(lambda k: 0) + tspecs(lambda k: jnp.minimum(k + 1, last)),
        out_specs=pl.BlockSpec((gc, nrows, CHUNK), lambda k: (k, 0, 0)),
        out_shape=jax.ShapeDtypeStruct((sw, nrows, CHUNK), F32),
        scratch_shapes=[pltpu.VMEM((2, gc * CHUNK, gc * CHUNK), BF16),
                        pltpu.VMEM((2, gc * CHUNK, 2 * ns), BF16),
                        pltpu.VMEM((2, gc * CHUNK, 2 * ns), BF16),
                        pltpu.VMEM((2, 2, 8, ns), F32),
                        pltpu.VMEM((2, nrows, ns), F32),
                        pltpu.VMEM((2, nrows, ns), F32),
                        pltpu.VMEM((2, nrows, ns), F32)],
        compiler_params=_cparams("arbitrary"),
        name="s5",
    )(xat3, *tabs, *tabs)


def _s5_tables(lam_re, lam_im, log_step, b_re, b_im, c_re, c_im, d):
    ng, npar, nc = b_re.shape[1], b_re.shape[2], b_re.shape[3]
    half = nc // 2
    lam_re, lam_im = lam_re.astype(F32), lam_im.astype(F32)
    step = jnp.exp(log_step.astype(F32))[..., None]
    mag = jnp.exp(lam_re * step)
    lb_re, lb_im = mag * jnp.cos(lam_im * step), mag * jnp.sin(lam_im * step)
    den = lam_re * lam_re + lam_im * lam_im
    q_re = ((lb_re - 1.0) * lam_re + lb_im * lam_im) / den
    q_im = (lb_im * lam_re - (lb_re - 1.0) * lam_im) / den
    lanes = lambda a: a.transpose(1, 0, 2).reshape(ng, 2 * npar)
    rows = jnp.stack([lanes(a) for a in (lb_re, lb_im, q_re, q_im)], axis=1)
    rows = jnp.pad(rows, ((0, 0), (0, 4), (0, 0)))
    ci = jnp.arange(nc)[:, None, None]
    co = 2 * jnp.arange(half)[None, :, None] + jnp.arange(2)[None, None, :]
    dsel = (ci == co).astype(F32).reshape(nc * half, 2)
    dcol = jnp.repeat(d.astype(F32).reshape(ng, nc), half, axis=1)[..., None] * dsel[None]
    cols = jnp.concatenate([rows[:, :2].transpose(0, 2, 1), dcol], axis=2)
    perm = jnp.concatenate([jnp.arange(0, nc, 2), jnp.arange(1, nc, 2)])
    bc = jnp.stack([b_re.astype(F32).transpose(1, 3, 0, 2).reshape(ng, nc, 2 * npar),
                    b_im.astype(F32).transpose(1, 3, 0, 2).reshape(ng, nc, 2 * npar),
                    c_re.astype(F32).transpose(1, 2, 0, 3).reshape(ng, nc, 2 * npar)[:, perm],
                    c_im.astype(F32).transpose(1, 2, 0, 3).reshape(ng, nc, 2 * npar)[:, perm]],
                   axis=1)
    return rows, cols, bc


def _merge_xattn_kernel(yt_ref, ga_ref, pb_ref, x_ref, wglut_ref, wpa_ref, wout_ref,
                        g_ref, wq_ref, k_ref, v_ref, wo_ref, o_ref):
    ts = TOKEN_SUB_WIDE

    def body(r0, sub):
        rows = pl.ds(r0, ts)
        ncs = ts // CHUNK
        y = jax.nn.gelu(jnp.concatenate([yt_ref[:, sub * ncs + c, :] for c in range(ncs)], axis=1))
        gl = _dot(wglut_ref[...], y.astype(BF16))
        ya = (y * jax.nn.sigmoid(gl)).T.astype(BF16)
        merged = ga_ref[rows, :].astype(F32) * _dot(ya, wpa_ref[...]) + pb_ref[rows, :].astype(F32)
        h = x_ref[rows, :] + _dot(merged.astype(BF16), wout_ref[...])

        hn = _rms(h, g_ref[...]).astype(BF16)
        q = _dot(hn, wq_ref[...]).astype(BF16)
        hd = q.shape[1] // XATTN_HEADS
        outs = []
        for hh in range(XATTN_HEADS):
            sl = slice(hh * hd, (hh + 1) * hd)
            s = _dot_nt(q[:, sl], k_ref[0, :, sl])
            e = jnp.exp(s - jnp.max(s, axis=-1, keepdims=True))
            p = e * (1.0 / jnp.sum(e, axis=-1, keepdims=True))
            outs.append(_dot(p.astype(BF16), v_ref[0, :, sl]))
        o = jnp.concatenate(outs, axis=1).astype(BF16)
        o_ref[rows, :] = h + _dot(o, wo_ref[...])

    _sub_loop(x_ref.shape[0], body, ts)


def _merge_xattn(yt3, ga, pb, x2, wglut, wpa, wout, g, wq, k, v, wo):
    t, d = x2.shape
    sw = yt3.shape[0]
    nb, m = k.shape[0], k.shape[1]
    tm = TOKEN_BLOCK
    per_batch = t // nb // tm
    tok = lambda i: (i, 0)
    mem = lambda i: (i // per_batch, 0, 0)
    consts = (wglut, wpa, wout, g, wq)
    return pl.pallas_call(
        _merge_xattn_kernel,
        grid=(t // tm,),
        in_specs=[pl.BlockSpec((sw, tm // CHUNK, CHUNK), lambda i: (0, i, 0)),
                  pl.BlockSpec((tm, d), tok), pl.BlockSpec((tm, d), tok), pl.BlockSpec((tm, d), tok)]
                 + [_const_spec(c.shape) for c in consts]
                 + [pl.BlockSpec((1, m, d), mem), pl.BlockSpec((1, m, d), mem), _const_spec(wo.shape)],
        out_specs=pl.BlockSpec((tm, d), tok),
        out_shape=jax.ShapeDtypeStruct((t, d), F32),
        compiler_params=_cparams("arbitrary"),
        name="merge_xattn",
    )(yt3, ga, pb, x2, *consts, k, v, wo)


def _ffn_kernel(h_ref, g_ref, wg_ref, wu_ref, wd_ref, gf_ref, o_ref):
    def body(r0, sub):
        rows = pl.ds(r0, TOKEN_SUB)
        h = h_ref[rows, :]
        hn = _rms(h, g_ref[...]).astype(BF16)
        act = (jax.nn.silu(_dot(hn, wg_ref[...])) * _dot(hn, wu_ref[...])).astype(BF16)
        o_ref[rows, :] = _rms(h + _dot(act, wd_ref[...]), gf_ref[...])

    _sub_loop(h_ref.shape[0], body)


def _ffn(h2, g, wg, wu, wd, gf):
    t, d = h2.shape
    tm = TOKEN_BLOCK
    consts = (g, wg, wu, wd, gf)
    return pl.pallas_call(
        _ffn_kernel,
        grid=(t // tm,),
        in_specs=[pl.BlockSpec((tm, d), lambda i: (i, 0))] + [_const_spec(c.shape) for c in consts],
        out_specs=pl.BlockSpec((tm, d), lambda i: (i, 0)),
        out_shape=jax.ShapeDtypeStruct((t, d), F32),
        compiler_params=_cparams("arbitrary"),
        name="ffn",
    )(h2, *consts)


def _layer(h, kmem, vmem, mix_g, w_in, s5p, w_glu, ln_g, ln_b, sgu_w, sgu_bias, wpa, wpb, wout,
           xg, wq, wxo, fg, wgate, wup, wdown, final_g):
    b, l, d = h.shape
    t = b * l
    sw = w_glu.shape[0]
    su = ln_g.shape[0]
    row = lambda a: a.astype(F32).reshape(1, -1)
    x2 = h.reshape(t, d)
    bias_full = jnp.repeat(sgu_bias.astype(F32).T, su // SGU_HEADS, axis=1)
    xat3, ga, pb = _in_proj(
        x2, row(mix_g), w_in[:, :sw].T.astype(BF16), w_in.astype(BF16),
        row(ln_g), row(ln_b), sgu_w.astype(BF16), bias_full, wpb.astype(BF16))
    yt3 = _s5(xat3, _s5_tables(*s5p), b)
    h2 = _merge_xattn(yt3, ga, pb, x2, w_glu.T.astype(BF16), wpa.astype(BF16), wout.astype(BF16),
                      row(xg), wq.astype(BF16), kmem, vmem, wxo.astype(BF16))
    out = _ffn(h2, row(fg), wgate.astype(BF16), wup.astype(BF16), wdown.astype(BF16),
               row(final_g))
    return out.reshape(b, l, d)


def kernel(x, mem, mix_norm_g, w_in, s5_lam_re, s5_lam_im, s5_log_step, s5_b_re, s5_b_im, s5_c_re, s5_c_im, s5_d, s5_w_glu, sgu_ln_g, sgu_ln_b, sgu_w, sgu_bias, w_proj_a, w_proj_b, w_out, xattn_norm_g, mem_norm_g, w_q, w_k, w_v, w_xo, ffn_norm_g, w_gate, w_up, w_down, final_norm_g):
    depth = w_in.shape[0]
    assert depth == 1, "the final rms_norm is fused into the last layer's ffn kernel"
    hd = x.shape[-1] // XATTN_HEADS
    i = 0
    kmem, vmem = _mem_kv(mem, mem_norm_g.astype(F32).reshape(1, -1), w_k[i].astype(BF16),
                         w_v[i].astype(BF16), hd ** -0.5)
    s5p = (s5_lam_re[i], s5_lam_im[i], s5_log_step[i], s5_b_re[i], s5_b_im[i], s5_c_re[i],
           s5_c_im[i], s5_d[i])
    return _layer(x, kmem, vmem, mix_norm_g[i], w_in[i], s5p, s5_w_glu[i], sgu_ln_g[i], sgu_ln_b[i],
                  sgu_w[i], sgu_bias[i], w_proj_a[i], w_proj_b[i], w_out[i], xattn_norm_g[i], w_q[i],
                  w_xo[i], ffn_norm_g[i], w_gate[i], w_up[i], w_down[i], final_norm_g)
```
